```python
import math
import jax, jax.numpy as jnp
from jax import lax
import numpy as np

D_MODEL = 1024
BATCH = 4
SEQ = 4096
DEPTH = 1

D_MIX = D_MODEL
ATTN_WIDTH = D_MIX // 2
CONV_WIDTH = D_MIX - ATTN_WIDTH
N_ATTN_HEADS = 4
ATTN_HEAD_DIM = ATTN_WIDTH // (2 * N_ATTN_HEADS)
CONV_GROUPS = 8
CONV_WIDTH_K = 3
PLE_DIM = 256
Q_BLOCK = 128
EPS = 1e-6
N_IN_COLS = 4 * ATTN_WIDTH + 4 * CONV_WIDTH

kernel_name = "hybrid_diffattn_shortconv_parallel_block"


def rmsnorm(x, g, eps=EPS):
    xf = x.astype(jnp.float32)
    y = xf * lax.rsqrt(jnp.mean(xf * xf, axis=-1, keepdims=True) + eps)
    return (y * g.astype(jnp.float32)).astype(x.dtype)


def alibi_slopes(n_heads):
    h = jnp.arange(1, n_heads + 1, dtype=jnp.float32)
    return jnp.exp2(-8.0 * h / n_heads)


def diff_attention(q1, q2, k1, k2, v, lam):
    b, n_h, s, d = q1.shape
    dv = v.shape[-1]
    n_blk = s // Q_BLOCK
    scale = d ** -0.5
    slopes = alibi_slopes(n_h)
    kpos = jnp.arange(s, dtype=jnp.int32)
    q1b = q1.reshape(b, n_h, n_blk, Q_BLOCK, d).transpose(2, 0, 1, 3, 4)
    q2b = q2.reshape(b, n_h, n_blk, Q_BLOCK, d).transpose(2, 0, 1, 3, 4)
    starts = jnp.arange(n_blk, dtype=jnp.int32) * Q_BLOCK

    def one_block(args):
        q1_blk, q2_blk, start = args
        qpos = start + jnp.arange(Q_BLOCK, dtype=jnp.int32)
        dist = jnp.abs(qpos[:, None] - kpos[None, :]).astype(jnp.float32)
        bias = -slopes[:, None, None] * dist[None]
        s1 = jnp.einsum('bhqd,bhkd->bhqk', q1_blk, k1).astype(jnp.float32) * scale + bias
        s2 = jnp.einsum('bhqd,bhkd->bhqk', q2_blk, k2).astype(jnp.float32) * scale + bias
        probs = jax.nn.softmax(s1, axis=-1) - lam * jax.nn.softmax(s2, axis=-1)
        return jnp.einsum('bhqk,bhkv->bhqv', probs.astype(v.dtype), v)

    o = lax.map(one_block, (q1b, q2b, starts))
    return o.transpose(1, 0, 3, 2, 4).reshape(b, s, n_h, dv)


def depthwise_conv3(u, w):
    c = u.shape[-1]
    rhs = w.astype(u.dtype)[:, None, :]
    pad = (CONV_WIDTH_K - 1) // 2
    return lax.conv_general_dilated(
        u, rhs, window_strides=(1,), padding=((pad, pad),),
        dimension_numbers=('NWC', 'WIO', 'NWC'), feature_group_count=c)


def setup_inputs(seed: int = 0) -> dict:
    key = jax.random.key(seed)
    ks = jax.random.split(key, 16)
    f32 = jnp.float32
    d = ATTN_HEAD_DIM
    x = jax.random.normal(ks[0], (BATCH, SEQ, D_MODEL), f32)
    p = jax.random.normal(ks[1], (DEPTH, BATCH, SEQ, PLE_DIM), f32)
    mix_norm_g = 1.0 + 0.02 * jax.random.normal(ks[2], (DEPTH, D_MODEL), f32)
    w_in = jax.random.normal(ks[3], (DEPTH, D_MODEL, N_IN_COLS), f32) * D_MODEL ** -0.5
    lambda_q1 = 0.1 * jax.random.normal(ks[4], (DEPTH, d), f32)
    lambda_k1 = 0.1 * jax.random.normal(ks[5], (DEPTH, d), f32)
    lambda_q2 = 0.1 * jax.random.normal(ks[6], (DEPTH, d), f32)
    lambda_k2 = 0.1 * jax.random.normal(ks[7], (DEPTH, d), f32)
    subln_g = 1.0 + 0.02 * jax.random.normal(ks[8], (DEPTH, 2 * d), f32)
    conv_w = jax.random.normal(ks[9], (DEPTH, CONV_WIDTH_K, CONV_WIDTH), f32) * CONV_WIDTH_K ** -0.5
    w_out = jax.random.normal(ks[10], (DEPTH, D_MIX, D_MODEL), f32) * D_MIX ** -0.5
    ple_norm_g = 1.0 + 0.02 * jax.random.normal(ks[11], (DEPTH, D_MODEL), f32)
    w_ple_gate = jax.random.normal(ks[12], (DEPTH, D_MODEL, D_MODEL), f32) * D_MODEL ** -0.5
    w_ple_proj = jax.random.normal(ks[13], (DEPTH, PLE_DIM, D_MODEL), f32) * PLE_DIM ** -0.5
    final_norm_g = 1.0 + 0.02 * jax.random.normal(ks[14], (D_MODEL,), f32)
    return {"x": x, "p": p, "mix_norm_g": mix_norm_g, "w_in": w_in,
            "lambda_q1": lambda_q1, "lambda_k1": lambda_k1,
            "lambda_q2": lambda_q2, "lambda_k2": lambda_k2,
            "subln_g": subln_g, "conv_w": conv_w, "w_out": w_out,
            "ple_norm_g": ple_norm_g, "w_ple_gate": w_ple_gate,
            "w_ple_proj": w_ple_proj, "final_norm_g": final_norm_g}


def reference(x, p, mix_norm_g, w_in, lambda_q1, lambda_k1, lambda_q2, lambda_k2,
              subln_g, conv_w, w_out, ple_norm_g, w_ple_gate, w_ple_proj, final_norm_g):
    b, s, _ = x.shape
    n_h, d = N_ATTN_HEADS, ATTN_HEAD_DIM
    split_at = [ATTN_WIDTH, 2 * ATTN_WIDTH, 3 * ATTN_WIDTH, 4 * ATTN_WIDTH,
                4 * ATTN_WIDTH + CONV_WIDTH, 4 * ATTN_WIDTH + 2 * CONV_WIDTH,
                4 * ATTN_WIDTH + 3 * CONV_WIDTH]
    for i in range(DEPTH):
        lambda_init = 0.8 - 0.6 * math.exp(-0.3 * i)
        h = rmsnorm(x, mix_norm_g[i])
        z = h @ w_in[i]
        q, k, v, g_attn, cb, cc, ch, g_conv = jnp.split(z, split_at, axis=-1)

        q = q.reshape(b, s, n_h, 2, d).transpose(0, 2, 3, 1, 4)
        k = k.reshape(b, s, n_h, 2, d).transpose(0, 2, 3, 1, 4)
        vh = v.reshape(b, s, n_h, 2 * d).transpose(0, 2, 1, 3)
        lam = (jnp.exp(jnp.sum(lambda_q1[i].astype(jnp.float32) * lambda_k1[i].astype(jnp.float32)))
               - jnp.exp(jnp.sum(lambda_q2[i].astype(jnp.float32) * lambda_k2[i].astype(jnp.float32)))
               + lambda_init)
        o_attn = diff_attention(q[:, :, 0], q[:, :, 1], k[:, :, 0], k[:, :, 1], vh, lam)
        o_attn = rmsnorm(o_attn, subln_g[i]) * (1.0 - lambda_init)
        o_attn = o_attn.reshape(b, s, ATTN_WIDTH) * jax.nn.silu(g_attn)

        o_conv = cb * depthwise_conv3(cc * ch, conv_w[i]) * jax.nn.silu(g_conv)

        x = x + jnp.concatenate([o_attn, o_conv], axis=-1) @ w_out[i]

        gate = jax.nn.sigmoid(rmsnorm(x, ple_norm_g[i]) @ w_ple_gate[i])
        x = x + gate * (p[i] @ w_ple_proj[i])
    return rmsnorm(x, final_norm_g)
```

```python
import functools
import math

import jax
import jax.numpy as jnp
from jax import lax
from jax.experimental import pallas as pl
from jax.experimental.pallas import tpu as pltpu

F32 = jnp.float32
BF16 = jnp.bfloat16

N_HEADS = 4
HEAD_DIM = 64
V_DIM = 2 * HEAD_DIM
ATTN_WIDTH = N_HEADS * V_DIM
CONV_WIDTH = 512
EPS = 1e-6

ROW_TILE = 512
Q_TILE = 512
K_TILE = 512
POS_SPLIT_BITS = 6
ONES_ROWS = 16
NEG_BIG = -1e30
VMEM_LIMIT_BYTES = 48 * 1024 * 1024


def _silu(v):
    return v * jax.nn.sigmoid(v)


def _inproj_kernel(x_ref, g_ref, wn_ref, wqt_ref, wvt_ref,
                   kk_ref, ga_ref, cg_ref, u_ref, qt_ref, vt_ref, *, gate_scale):
    x = x_ref[...]
    ms = jnp.mean(x * x, axis=-1, keepdims=True)
    h = (x * lax.rsqrt(ms + EPS) * g_ref[...]).astype(BF16)
    zn = jnp.dot(h, wn_ref[...], preferred_element_type=F32)
    w = ATTN_WIDTH
    kk_ref[...] = zn[:, 0:w].astype(BF16)
    ga_ref[...] = _silu(zn[:, w:2 * w]) * gate_scale
    cb = zn[:, 2 * w:3 * w]
    cc = zn[:, 3 * w:4 * w]
    ch = zn[:, 4 * w:5 * w]
    gc = zn[:, 5 * w:6 * w]
    cg_ref[...] = cb * _silu(gc)
    u_ref[...] = cc * ch
    nt = (((1,), (1,)), ((), ()))
    qt_ref[0] = lax.dot_general(wqt_ref[...], h, nt, preferred_element_type=F32).astype(BF16)
    vt_ref[0] = lax.dot_general(wvt_ref[...], h, nt, preferred_element_type=F32).astype(BF16)


def _inproj(xf, g, wn, wqt, wvt, gate_scale):
    n, d = xf.shape
    tm = ROW_TILE
    nt = n // tm
    w = ATTN_WIDTH
    row_spec = lambda width: pl.BlockSpec((tm, width), lambda i: (i, 0))
    full = lambda a: pl.BlockSpec(a.shape, lambda i: (0,) * a.ndim)
    t_spec = pl.BlockSpec((1, w, tm), lambda i: (i, 0, 0))
    return pl.pallas_call(
        functools.partial(_inproj_kernel, gate_scale=gate_scale),
        grid=(nt,),
        in_specs=[row_spec(d), full(g), full(wn), full(wqt), full(wvt)],
        out_specs=[row_spec(w), row_spec(w), row_spec(w), row_spec(w), t_spec, t_spec],
        out_shape=[
            jax.ShapeDtypeStruct((n, w), BF16),
            jax.ShapeDtypeStruct((n, w), F32),
            jax.ShapeDtypeStruct((n, w), F32),
            jax.ShapeDtypeStruct((n, w), F32),
            jax.ShapeDtypeStruct((nt, w, tm), BF16),
            jax.ShapeDtypeStruct((nt, w, tm), BF16),
        ],
        compiler_params=pltpu.CompilerParams(
            dimension_semantics=("arbitrary",), vmem_limit_bytes=VMEM_LIMIT_BYTES),
        name="inproj",
    )(xf, g, wn, wqt, wvt)


LEFT, DIAG, RIGHT = 0, 1, 2
SIGMA = {LEFT: 1.0, DIAG: 0.0, RIGHT: -1.0}


def _attn_kernel(qt_ref, kk_ref, vt_ref, ga_ref, sg_ref, lq1_ref, lk1_ref, lq2_ref, lk2_ref,
                 o_ref, pos_ref, w_ref, bias_ref, acc1_ref, acc2_ref, *, lambda_init):
    n_q = qt_ref.shape[1]
    n_k = vt_ref.shape[1]
    tq, tk = Q_TILE, K_TILE
    head = pl.program_id(1)

    slope_bits = (127 - 2 * (head + 1)) << 23
    slope = lax.bitcast_convert_type(jnp.full((1, 1), slope_bits, jnp.int32), F32)

    lam = (jnp.exp(jnp.sum(lq1_ref[...] * lk1_ref[...], axis=-1, keepdims=True))
           - jnp.exp(jnp.sum(lq2_ref[...] * lk2_ref[...], axis=-1, keepdims=True))
           + lambda_init)

    s_len = pos_ref.shape[0]
    kpos = lax.broadcasted_iota(jnp.int32, (s_len, 128), 0)
    lane = lax.broadcasted_iota(jnp.int32, (s_len, 128), 1)
    hi = ((kpos >> POS_SPLIT_BITS) << POS_SPLIT_BITS).astype(F32)
    lo = (kpos & ((1 << POS_SPLIT_BITS) - 1)).astype(F32)
    pos_ref[...] = jnp.where(lane == 0, hi, jnp.where(lane == 1, lo, 0.0)).astype(BF16)

    di = lax.broadcasted_iota(jnp.int32, (tk, tq), 0) - lax.broadcasted_iota(jnp.int32, (tk, tq), 1)
    bias_ref[...] = -slope * jnp.abs(di).astype(F32)

    ones_rows = jnp.where(lax.broadcasted_iota(jnp.int32, (ONES_ROWS, tk), 0) == 0, 1.0, 0.0).astype(BF16)
    row128 = lax.broadcasted_iota(jnp.int32, (128, tq), 0)
    qlane = lax.broadcasted_iota(jnp.int32, (1, tq), 1)

    def q_tile_body(qi, _):
        q0 = pl.multiple_of(qi * tq, tq)
        qq = qt_ref[0, qi].astype(F32)
        tops = (jnp.where(row128 < HEAD_DIM, qq, 0.0).astype(BF16),
                jnp.where(row128 >= HEAD_DIM, qq, 0.0).astype(BF16))
        for v in (LEFT, DIAG, RIGHT):
            bottom = jnp.where(row128 < 2, SIGMA[v] * slope, 0.0).astype(BF16)
            for t in range(2):
                w_ref[v, t, 0:128, :] = tops[t]
                w_ref[v, t, 128:256, :] = bottom
        acc1_ref[...] = jnp.zeros_like(acc1_ref)
        acc2_ref[...] = jnp.zeros_like(acc2_ref)
        qpos = (qlane + q0).astype(F32) * slope
        shift = {LEFT: -qpos, DIAG: jnp.zeros_like(qpos), RIGHT: qpos}

        def chunk(c, carry, v):
            k0 = pl.multiple_of(c * tk, tk)
            lhs = jnp.concatenate([kk_ref[0, pl.ds(k0, tk), :], pos_ref[pl.ds(k0, tk), :]], axis=1)
            vt = jnp.concatenate([vt_ref[0, c], ones_rows], axis=0)
            new = []
            for t, acc_ref in enumerate((acc1_ref, acc2_ref)):
                m_old = carry[t]
                s = jnp.dot(lhs, w_ref[v, t], preferred_element_type=F32)
                if v == DIAG:
                    s = s + bias_ref[...]
                m_new = jnp.maximum(m_old, jnp.max(s, axis=0, keepdims=True) + shift[v])
                alpha = jnp.exp(m_old - m_new)
                p = jnp.exp(s - (m_new - shift[v])).astype(BF16)
                pv = jnp.dot(vt, p, preferred_element_type=F32)
                acc_ref[...] = alpha * acc_ref[...] + pv
                new.append(m_new)
            return tuple(new)

        m0 = jnp.full((1, tq), NEG_BIG, F32)
        carry = (m0, m0)
        carry = lax.fori_loop(0, qi, functools.partial(chunk, v=LEFT), carry)
        carry = chunk(qi, carry, DIAG)
        carry = lax.fori_loop(qi + 1, n_k, functools.partial(chunk, v=RIGHT), carry)

        a1 = acc1_ref[...]
        a2 = acc2_ref[...]
        att = a1[0:V_DIM] / a1[V_DIM:V_DIM + 1] - lam * (a2[0:V_DIM] / a2[V_DIM:V_DIM + 1])
        ms = jnp.mean(att * att, axis=0, keepdims=True)
        y = att * lax.rsqrt(ms + EPS) * sg_ref[...]
        o_ref[0, pl.ds(q0, tq), :] = (y.T * ga_ref[0, pl.ds(q0, tq), :]).astype(BF16)
        return 0

    lax.fori_loop(0, n_q, q_tile_body, 0)


def _attention(qt, kk, vt, ga, sg, lq1, lk1, lq2, lk2, lambda_init):
    b, n_q, _, tq = qt.shape
    _, n_k, _, tk = vt.shape
    s = kk.shape[1]
    assert tq == Q_TILE and tk == K_TILE and n_q * tq == s and n_k * tk == s
    assert s <= 1 << (POS_SPLIT_BITS + 8)
    full = lambda a: pl.BlockSpec(a.shape, lambda bi, hi: (0,) * a.ndim)
    t_spec = lambda nblk, width: pl.BlockSpec((1, nblk, V_DIM, width), lambda bi, hi: (bi, 0, hi, 0))
    r_spec = pl.BlockSpec((1, s, V_DIM), lambda bi, hi: (bi, 0, hi))
    return pl.pallas_call(
        functools.partial(_attn_kernel, lambda_init=lambda_init),
        grid=(b, N_HEADS),
        in_specs=[t_spec(n_q, tq), r_spec, t_spec(n_k, tk), r_spec,
                  full(sg), full(lq1), full(lk1), full(lq2), full(lk2)],
        out_specs=r_spec,
        out_shape=jax.ShapeDtypeStruct((b, s, ATTN_WIDTH), BF16),
        scratch_shapes=[
            pltpu.VMEM((s, 128), BF16),
            pltpu.VMEM((3, 2, 256, Q_TILE), BF16),
            pltpu.VMEM((K_TILE, Q_TILE), F32),
            pltpu.VMEM((V_DIM + ONES_ROWS, Q_TILE), F32),
            pltpu.VMEM((V_DIM + ONES_ROWS, Q_TILE), F32),
        ],
        compiler_params=pltpu.CompilerParams(
            dimension_semantics=("arbitrary", "arbitrary"), vmem_limit_bytes=VMEM_LIMIT_BYTES),
        name="diff_attn",
    )(qt, kk, vt, ga, sg, lq1, lk1, lq2, lk2)


def _outproj_kernel(x_ref, oa_ref, cg_ref, u_ref, up_ref, un_ref, p_ref, cw_ref, wo_ref,
                    g2_ref, wg_ref, wp_ref, gf_ref, out_ref, *, tiles_per_seq, final_norm):
    i = pl.program_id(0)
    tm = x_ref.shape[0]
    u = u_ref[...]
    row = lax.broadcasted_iota(jnp.int32, (tm, 1), 0)
    t_in_seq = i % tiles_per_seq
    prev_row = jnp.where(t_in_seq == 0, 0.0, up_ref[7:8, :])
    next_row = jnp.where(t_in_seq == tiles_per_seq - 1, 0.0, un_ref[0:1, :])
    u_prev = jnp.where(row == 0, prev_row, pltpu.roll(u, 1, axis=0))
    u_next = jnp.where(row == tm - 1, next_row, pltpu.roll(u, tm - 1, axis=0))
    cw = cw_ref[...]
    conv = cw[0:1] * u_prev + cw[1:2] * u + cw[2:3] * u_next
    o_conv = (cg_ref[...] * conv).astype(BF16)
    aw = oa_ref.shape[1]
    mix = (jnp.dot(oa_ref[...], wo_ref[0:aw, :], preferred_element_type=F32)
           + jnp.dot(o_conv, wo_ref[aw:, :], preferred_element_type=F32))
    x1 = x_ref[...] + mix
    ms1 = jnp.mean(x1 * x1, axis=-1, keepdims=True)
    n1 = (x1 * lax.rsqrt(ms1 + EPS) * g2_ref[...]).astype(BF16)
    gate = jax.nn.sigmoid(jnp.dot(n1, wg_ref[...], preferred_element_type=F32))
    pp = jnp.dot(p_ref[...].astype(BF16), wp_ref[...], preferred_element_type=F32)
    x2 = x1 + gate * pp
    if final_norm:
        ms2 = jnp.mean(x2 * x2, axis=-1, keepdims=True)
        x2 = x2 * lax.rsqrt(ms2 + EPS) * gf_ref[...]
    out_ref[...] = x2


def _outproj(xf, oa, cg, u, pf, cw, wo, g2, wg, wp, gf, seq_len, final_norm):
    n, d = xf.shape
    tm = ROW_TILE
    nt = n // tm
    sub = 8
    nsub = n // sub
    row_spec = lambda width: pl.BlockSpec((tm, width), lambda i: (i, 0))
    full = lambda a: pl.BlockSpec(a.shape, lambda i: (0,) * a.ndim)
    prev_spec = pl.BlockSpec((sub, CONV_WIDTH), lambda i: (jnp.maximum(i * (tm // sub) - 1, 0), 0))
    next_spec = pl.BlockSpec((sub, CONV_WIDTH), lambda i: (jnp.minimum((i + 1) * (tm // sub), nsub - 1), 0))
    return pl.pallas_call(
        functools.partial(_outproj_kernel, tiles_per_seq=seq_len // tm, final_norm=final_norm),
        grid=(nt,),
        in_specs=[row_spec(d), row_spec(ATTN_WIDTH), row_spec(CONV_WIDTH), row_spec(CONV_WIDTH),
                  prev_spec, next_spec, row_spec(pf.shape[1]),
                  full(cw), full(wo), full(g2), full(wg), full(wp), full(gf)],
        out_specs=row_spec(d),
        out_shape=jax.ShapeDtypeStruct((n, d), F32),
        compiler_params=pltpu.CompilerParams(
            dimension_semantics=("arbitrary",), vmem_limit_bytes=VMEM_LIMIT_BYTES),
        name="outproj",
    )(xf, oa, cg, u, u, u, pf, cw, wo, g2, wg, wp, gf)


def kernel(x, p, mix_norm_g, w_in, lambda_q1, lambda_k1, lambda_q2, lambda_k2, subln_g, conv_w,
           w_out, ple_norm_g, w_ple_gate, w_ple_proj, final_norm_g):
    b, s, d = x.shape
    depth = w_in.shape[0]
    assert s % ROW_TILE == 0 and ROW_TILE == Q_TILE == K_TILE
    aw = ATTN_WIDTH
    xf = x.reshape(b * s, d)
    gf = final_norm_g.reshape(1, d)
    for i in range(depth):
        lambda_init = 0.8 - 0.6 * math.exp(-0.3 * i)
        wi = w_in[i]
        wn = jnp.concatenate([wi[:, aw:2 * aw], wi[:, 3 * aw:]], axis=1).astype(BF16)
        wqt = (wi[:, 0:aw] * HEAD_DIM ** -0.5).T.astype(BF16)
        wvt = wi[:, 2 * aw:3 * aw].T.astype(BF16)
        kk, ga, cg, u, qt, vt = _inproj(xf, mix_norm_g[i].reshape(1, d), wn, wqt, wvt,
                                        1.0 - lambda_init)
        n_t = s // ROW_TILE
        oa = _attention(qt.reshape(b, n_t, aw, ROW_TILE), kk.reshape(b, s, aw),
                        vt.reshape(b, n_t, aw, ROW_TILE), ga.reshape(b, s, aw),
                        subln_g[i].reshape(V_DIM, 1),
                        lambda_q1[i].reshape(1, -1), lambda_k1[i].reshape(1, -1),
                        lambda_q2[i].reshape(1, -1), lambda_k2[i].reshape(1, -1), lambda_init)
        xf = _outproj(xf, oa.reshape(b * s, aw), cg, u, p[i].reshape(b * s, -1), conv_w[i],
                      w_out[i].astype(BF16), ple_norm_g[i].reshape(1, d),
                      w_ple_gate[i].astype(BF16), w_ple_proj[i].astype(BF16), gf, s,
                      final_norm=(i == depth - 1))
    return xf.reshape(b, s, d)
```

```python
import functools
import math

import jax
import jax.numpy as jnp
from jax import lax
from jax.experimental import pallas as pl
from jax.experimental.pallas import tpu as pltpu

F32 = jnp.float32
BF16 = jnp.bfloat16

N_HEADS = 4
HEAD_DIM = 64
V_DIM = 2 * HEAD_DIM
ATTN_WIDTH = N_HEADS * V_DIM
CONV_WIDTH = 512
EPS = 1e-6

ROW_TILE = 512
Q_TILE = 512
K_TILE = 512
POS_SPLIT_BITS = 6
ONES_ROWS = 16
VMEM_LIMIT_BYTES = 48 * 1024 * 1024


def _silu(v):
    return v * jax.nn.sigmoid(v)


def _inproj_kernel(x_ref, g_ref, wn_ref, wqt_ref, wvt_ref,
                   kk_ref, ga_ref, cg_ref, u_ref, qt_ref, vt_ref, *, gate_scale):
    x = x_ref[...]
    ms = jnp.mean(x * x, axis=-1, keepdims=True)
    h = (x * lax.rsqrt(ms + EPS) * g_ref[...]).astype(BF16)
    zn = jnp.dot(h, wn_ref[...], preferred_element_type=F32)
    w = ATTN_WIDTH
    kk_ref[...] = zn[:, 0:w].astype(BF16)
    ga_ref[...] = _silu(zn[:, w:2 * w]) * gate_scale
    cb = zn[:, 2 * w:3 * w]
    cc = zn[:, 3 * w:4 * w]
    ch = zn[:, 4 * w:5 * w]
    gc = zn[:, 5 * w:6 * w]
    cg_ref[...] = cb * _silu(gc)
    u_ref[...] = cc * ch
    nt = (((1,), (1,)), ((), ()))
    qt_ref[0] = lax.dot_general(wqt_ref[...], h, nt, preferred_element_type=F32).astype(BF16)
    vt_ref[0] = lax.dot_general(wvt_ref[...], h, nt, preferred_element_type=F32).astype(BF16)


def _inproj(xf, g, wn, wqt, wvt, gate_scale):
    n, d = xf.shape
    tm = ROW_TILE
    nt = n // tm
    w = ATTN_WIDTH
    row_spec = lambda width: pl.BlockSpec((tm, width), lambda i: (i, 0))
    full = lambda a: pl.BlockSpec(a.shape, lambda i: (0,) * a.ndim)
    t_spec = pl.BlockSpec((1, w, tm), lambda i: (i, 0, 0))
    return pl.pallas_call(
        functools.partial(_inproj_kernel, gate_scale=gate_scale),
        grid=(nt,),
        in_specs=[row_spec(d), full(g), full(wn), full(wqt), full(wvt)],
        out_specs=[row_spec(w), row_spec(w), row_spec(w), row_spec(w), t_spec, t_spec],
        out_shape=[
            jax.ShapeDtypeStruct((n, w), BF16),
            jax.ShapeDtypeStruct((n, w), F32),
            jax.ShapeDtypeStruct((n, w), F32),
            jax.ShapeDtypeStruct((n, w), F32),
            jax.ShapeDtypeStruct((nt, w, tm), BF16),
            jax.ShapeDtypeStruct((nt, w, tm), BF16),
        ],
        compiler_params=pltpu.CompilerParams(
            dimension_semantics=("arbitrary",), vmem_limit_bytes=VMEM_LIMIT_BYTES),
        name="inproj",
    )(xf, g, wn, wqt, wvt)


LEFT, DIAG, RIGHT = 0, 1, 2
SIGMA = {LEFT: 1.0, DIAG: 0.0, RIGHT: -1.0}


def _attn_kernel(qt_ref, kk_ref, vt_ref, ga_ref, sg_ref, lq1_ref, lk1_ref, lq2_ref, lk2_ref,
                 o_ref, pos_ref, w_ref, bias_ref, s_ref, acc1_ref, acc2_ref, *, lambda_init):
    n_q = qt_ref.shape[1]
    n_k = vt_ref.shape[1]
    tq, tk = Q_TILE, K_TILE
    head = pl.program_id(1)

    slope_bits = (127 - 2 * (head + 1)) << 23
    slope = lax.bitcast_convert_type(jnp.full((1, 1), slope_bits, jnp.int32), F32)

    lam = (jnp.exp(jnp.sum(lq1_ref[...] * lk1_ref[...], axis=-1, keepdims=True))
           - jnp.exp(jnp.sum(lq2_ref[...] * lk2_ref[...], axis=-1, keepdims=True))
           + lambda_init)

    s_len = pos_ref.shape[0]
    kpos = lax.broadcasted_iota(jnp.int32, (s_len, 128), 0)
    lane = lax.broadcasted_iota(jnp.int32, (s_len, 128), 1)
    hi = ((kpos >> POS_SPLIT_BITS) << POS_SPLIT_BITS).astype(F32)
    lo = (kpos & ((1 << POS_SPLIT_BITS) - 1)).astype(F32)
    pos_ref[...] = jnp.where(lane == 0, hi, jnp.where(lane == 1, lo, 0.0)).astype(BF16)

    di = lax.broadcasted_iota(jnp.int32, (tk, tq), 0) - lax.broadcasted_iota(jnp.int32, (tk, tq), 1)
    bias_ref[...] = -slope * jnp.abs(di).astype(F32)

    ones_rows = jnp.where(lax.broadcasted_iota(jnp.int32, (ONES_ROWS, tk), 0) == 0, 1.0, 0.0).astype(BF16)
    row128 = lax.broadcasted_iota(jnp.int32, (128, tq), 0)
    qlane = lax.broadcasted_iota(jnp.int32, (1, tq), 1)

    def q_tile_body(qi, _):
        q0 = pl.multiple_of(qi * tq, tq)
        qq = qt_ref[0, qi].astype(F32)
        tops = (jnp.where(row128 < HEAD_DIM, qq, 0.0).astype(BF16),
                jnp.where(row128 >= HEAD_DIM, qq, 0.0).astype(BF16))
        for v in (LEFT, DIAG, RIGHT):
            bottom = jnp.where(row128 < 2, SIGMA[v] * slope, 0.0).astype(BF16)
            for t in range(2):
                w_ref[v, t, 0:128, :] = tops[t]
                w_ref[v, t, 128:256, :] = bottom
        qpos = (qlane + q0).astype(F32) * slope

        def chunk_of(j):
            return j + (j >= qi).astype(jnp.int32)

        def shift_of(c):
            return jnp.where(c < qi, -1.0, 1.0) * qpos

        def scores(c, slot):
            k0 = pl.multiple_of(c * tk, tk)
            v = jnp.where(c < qi, LEFT, RIGHT)
            lhs = jnp.concatenate([kk_ref[0, pl.ds(k0, tk), :], pos_ref[pl.ds(k0, tk), :]], axis=1)
            shift = shift_of(c)
            cms = []
            for t in range(2):
                s = jnp.dot(lhs, w_ref[v, t], preferred_element_type=F32)
                s_ref[slot, t] = s
                cms.append(jnp.max(s, axis=0, keepdims=True) + shift)
            return tuple(cms)

        def softmax_pv(c, slot, m_run, cms):
            vt = jnp.concatenate([vt_ref[0, c], ones_rows], axis=0)
            shift = shift_of(c)
            new = []
            for t, acc_ref in enumerate((acc1_ref, acc2_ref)):
                m_new = jnp.maximum(m_run[t], cms[t])
                alpha = jnp.exp(m_run[t] - m_new)
                p = jnp.exp(s_ref[slot, t] - (m_new - shift)).astype(BF16)
                pv = jnp.dot(vt, p, preferred_element_type=F32)
                acc_ref[...] = alpha * acc_ref[...] + pv
                new.append(m_new)
            return tuple(new)

        cms_next = scores(chunk_of(0), 0)
        kd = pl.multiple_of(qi * tk, tk)
        lhs_d = jnp.concatenate([kk_ref[0, pl.ds(kd, tk), :], pos_ref[pl.ds(kd, tk), :]], axis=1)
        vt_d = jnp.concatenate([vt_ref[0, qi], ones_rows], axis=0)
        m_run = []
        for t, acc_ref in enumerate((acc1_ref, acc2_ref)):
            s = jnp.dot(lhs_d, w_ref[DIAG, t], preferred_element_type=F32) + bias_ref[...]
            m = jnp.max(s, axis=0, keepdims=True)
            p = jnp.exp(s - m).astype(BF16)
            acc_ref[...] = jnp.dot(vt_d, p, preferred_element_type=F32)
            m_run.append(m)
        m_run = tuple(m_run)

        def pipe(jj, carry):
            m_run, cms = carry
            j = 2 * jj
            cms_odd = scores(chunk_of(j + 1), 1)
            m_run = softmax_pv(chunk_of(j), 0, m_run, cms)
            cms_even = scores(chunk_of(j + 2), 0)
            m_run = softmax_pv(chunk_of(j + 1), 1, m_run, cms_odd)
            return m_run, cms_even

        n_rest = n_k - 1
        assert n_rest % 2 == 1
        m_run, cms_last = lax.fori_loop(0, n_rest // 2, pipe, (m_run, cms_next))
        softmax_pv(chunk_of(n_rest - 1), 0, m_run, cms_last)

        a1 = acc1_ref[...]
        a2 = acc2_ref[...]
        att = a1[0:V_DIM] / a1[V_DIM:V_DIM + 1] - lam * (a2[0:V_DIM] / a2[V_DIM:V_DIM + 1])
        ms = jnp.mean(att * att, axis=0, keepdims=True)
        y = att * lax.rsqrt(ms + EPS) * sg_ref[...]
        o_ref[0, pl.ds(q0, tq), :] = (y.T * ga_ref[0, pl.ds(q0, tq), :]).astype(BF16)
        return 0

    lax.fori_loop(0, n_q, q_tile_body, 0)


def _attention(qt, kk, vt, ga, sg, lq1, lk1, lq2, lk2, lambda_init):
    b, n_q, _, tq = qt.shape
    _, n_k, _, tk = vt.shape
    s = kk.shape[1]
    assert tq == Q_TILE and tk == K_TILE and n_q * tq == s and n_k * tk == s
    assert s <= 1 << (POS_SPLIT_BITS + 8)
    full = lambda a: pl.BlockSpec(a.shape, lambda bi, hi: (0,) * a.ndim)
    t_spec = lambda nblk, width: pl.BlockSpec((1, nblk, V_DIM, width), lambda bi, hi: (bi, 0, hi, 0))
    r_spec = pl.BlockSpec((1, s, V_DIM), lambda bi, hi: (bi, 0, hi))
    return pl.pallas_call(
        functools.partial(_attn_kernel, lambda_init=lambda_init),
        grid=(b, N_HEADS),
        in_specs=[t_spec(n_q, tq), r_spec, t_spec(n_k, tk), r_spec,
                  full(sg), full(lq1), full(lk1), full(lq2), full(lk2)],
        out_specs=r_spec,
        out_shape=jax.ShapeDtypeStruct((b, s, ATTN_WIDTH), BF16),
        scratch_shapes=[
            pltpu.VMEM((s, 128), BF16),
            pltpu.VMEM((3, 2, 256, Q_TILE), BF16),
            pltpu.VMEM((K_TILE, Q_TILE), F32),
            pltpu.VMEM((2, 2, K_TILE, Q_TILE), F32),
            pltpu.VMEM((V_DIM + ONES_ROWS, Q_TILE), F32),
            pltpu.VMEM((V_DIM + ONES_ROWS, Q_TILE), F32),
        ],
        compiler_params=pltpu.CompilerParams(
            dimension_semantics=("arbitrary", "arbitrary"), vmem_limit_bytes=VMEM_LIMIT_BYTES),
        name="diff_attn",
    )(qt, kk, vt, ga, sg, lq1, lk1, lq2, lk2)


def _outproj_kernel(x_ref, oa_ref, cg_ref, u_ref, up_ref, un_ref, p_ref, cw_ref, wo_ref,
                    g2_ref, wg_ref, wp_ref, gf_ref, out_ref, *, tiles_per_seq, final_norm):
    i = pl.program_id(0)
    tm = x_ref.shape[0]
    u = u_ref[...]
    row = lax.broadcasted_iota(jnp.int32, (tm, 1), 0)
    t_in_seq = i % tiles_per_seq
    prev_row = jnp.where(t_in_seq == 0, 0.0, up_ref[7:8, :])
    next_row = jnp.where(t_in_seq == tiles_per_seq - 1, 0.0, un_ref[0:1, :])
    u_prev = jnp.where(row == 0, prev_row, pltpu.roll(u, 1, axis=0))
    u_next = jnp.where(row == tm - 1, next_row, pltpu.roll(u, tm - 1, axis=0))
    cw = cw_ref[...]
    conv = cw[0:1] * u_prev + cw[1:2] * u + cw[2:3] * u_next
    o_conv = (cg_ref[...] * conv).astype(BF16)
    aw = oa_ref.shape[1]
    mix = (jnp.dot(oa_ref[...], wo_ref[0:aw, :], preferred_element_type=F32)
           + jnp.dot(o_conv, wo_ref[aw:, :], preferred_element_type=F32))
    x1 = x_ref[...] + mix
    ms1 = jnp.mean(x1 * x1, axis=-1, keepdims=True)
    n1 = (x1 * lax.rsqrt(ms1 + EPS) * g2_ref[...]).astype(BF16)
    gate = jax.nn.sigmoid(jnp.dot(n1, wg_ref[...], preferred_element_type=F32))
    pp = jnp.dot(p_ref[...].astype(BF16), wp_ref[...], preferred_element_type=F32)
    x2 = x1 + gate * pp
    if final_norm:
        ms2 = jnp.mean(x2 * x2, axis=-1, keepdims=True)
        x2 = x2 * lax.rsqrt(ms2 + EPS) * gf_ref[...]
    out_ref[...] = x2


def _outproj(xf, oa, cg, u, pf, cw, wo, g2, wg, wp, gf, seq_len, final_norm):
    n, d = xf.shape
    tm = ROW_TILE
    nt = n // tm
    sub = 8
    nsub = n // sub
    row_spec = lambda width: pl.BlockSpec((tm, width), lambda i: (i, 0))
    full = lambda a: pl.BlockSpec(a.shape, lambda i: (0,) * a.ndim)
    prev_spec = pl.BlockSpec((sub, CONV_WIDTH), lambda i: (jnp.maximum(i * (tm // sub) - 1, 0), 0))
    next_spec = pl.BlockSpec((sub, CONV_WIDTH), lambda i: (jnp.minimum((i + 1) * (tm // sub), nsub - 1), 0))
    return pl.pallas_call(
        functools.partial(_outproj_kernel, tiles_per_seq=seq_len // tm, final_norm=final_norm),
        grid=(nt,),
        in_specs=[row_spec(d), row_spec(ATTN_WIDTH), row_spec(CONV_WIDTH), row_spec(CONV_WIDTH),
                  prev_spec, next_spec, row_spec(pf.shape[1]),
                  full(cw), full(wo), full(g2), full(wg), full(wp), full(gf)],
        out_specs=row_spec(d),
        out_shape=jax.ShapeDtypeStruct((n, d), F32),
        compiler_params=pltpu.CompilerParams(
            dimension_semantics=("arbitrary",), vmem_limit_bytes=VMEM_LIMIT_BYTES),
        name="outproj",
    )(xf, oa, cg, u, u, u, pf, cw, wo, g2, wg, wp, gf)


def kernel(x, p, mix_norm_g, w_in, lambda_q1, lambda_k1, lambda_q2, lambda_k2, subln_g, conv_w,
           w_out, ple_norm_g, w_ple_gate, w_ple_proj, final_norm_g):
    b, s, d = x.shape
    depth = w_in.shape[0]
    assert s % ROW_TILE == 0 and ROW_TILE == Q_TILE == K_TILE
    aw = ATTN_WIDTH
    xf = x.reshape(b * s, d)
    gf = final_norm_g.reshape(1, d)
    for i in range(depth):
        lambda_init = 0.8 - 0.6 * math.exp(-0.3 * i)
        wi = w_in[i]
        wn = jnp.concatenate([wi[:, aw:2 * aw], wi[:, 3 * aw:]], axis=1).astype(BF16)
        wqt = (wi[:, 0:aw] * HEAD_DIM ** -0.5).T.astype(BF16)
        wvt = wi[:, 2 * aw:3 * aw].T.astype(BF16)
        kk, ga, cg, u, qt, vt = _inproj(xf, mix_norm_g[i].reshape(1, d), wn, wqt, wvt,
                                        1.0 - lambda_init)
        n_t = s // ROW_TILE
        oa = _attention(qt.reshape(b, n_t, aw, ROW_TILE), kk.reshape(b, s, aw),
                        vt.reshape(b, n_t, aw, ROW_TILE), ga.reshape(b, s, aw),
                        subln_g[i].reshape(V_DIM, 1),
                        lambda_q1[i].reshape(1, -1), lambda_k1[i].reshape(1, -1),
                        lambda_q2[i].reshape(1, -1), lambda_k2[i].reshape(1, -1), lambda_init)
        xf = _outproj(xf, oa.reshape(b * s, aw), cg, u, p[i].reshape(b * s, -1), conv_w[i],
                      w_out[i].astype(BF16), ple_norm_g[i].reshape(1, d),
                      w_ple_gate[i].astype(BF16), w_ple_proj[i].astype(BF16), gf, s,
                      final_norm=(i == depth - 1))
    return xf.reshape(b, s, d)
```

```python
import functools
import math

import jax
import jax.numpy as jnp
from jax import lax
from jax.experimental import pallas as pl
from jax.experimental.pallas import tpu as pltpu

F32 = jnp.float32
BF16 = jnp.bfloat16

N_HEADS = 4
HEAD_DIM = 64
V_DIM = 2 * HEAD_DIM
ATTN_WIDTH = N_HEADS * V_DIM
CONV_WIDTH = 512
EPS = 1e-6

ROW_TILE = 512
Q_TILE = 512
K_TILE = 512
POS_SPLIT_BITS = 6
ONES_ROWS = 16
NEG_BIG = -1e30
VMEM_LIMIT_BYTES = 48 * 1024 * 1024


def _silu(v):
    return v * jax.nn.sigmoid(v)


def _inproj_kernel(x_ref, g_ref, wn_ref, wqt_ref, wvt_ref,
                   kk_ref, ga_ref, cg_ref, u_ref, qt_ref, vt_ref, *, gate_scale):
    x = x_ref[...]
    ms = jnp.mean(x * x, axis=-1, keepdims=True)
    h = (x * lax.rsqrt(ms + EPS) * g_ref[...]).astype(BF16)
    zn = jnp.dot(h, wn_ref[...], preferred_element_type=F32)
    w = ATTN_WIDTH
    kk_ref[...] = zn[:, 0:w].astype(BF16)
    ga_ref[...] = _silu(zn[:, w:2 * w]) * gate_scale
    cb = zn[:, 2 * w:3 * w]
    cc = zn[:, 3 * w:4 * w]
    ch = zn[:, 4 * w:5 * w]
    gc = zn[:, 5 * w:6 * w]
    cg_ref[...] = cb * _silu(gc)
    u_ref[...] = cc * ch
    nt = (((1,), (1,)), ((), ()))
    qt_ref[0] = lax.dot_general(wqt_ref[...], h, nt, preferred_element_type=F32).astype(BF16)
    vt_ref[0] = lax.dot_general(wvt_ref[...], h, nt, preferred_element_type=F32).astype(BF16)


def _inproj(xf, g, wn, wqt, wvt, gate_scale):
    n, d = xf.shape
    tm = ROW_TILE
    nt = n // tm
    w = ATTN_WIDTH
    row_spec = lambda width: pl.BlockSpec((tm, width), lambda i: (i, 0))
    full = lambda a: pl.BlockSpec(a.shape, lambda i: (0,) * a.ndim)
    t_spec = pl.BlockSpec((1, w, tm), lambda i: (i, 0, 0))
    return pl.pallas_call(
        functools.partial(_inproj_kernel, gate_scale=gate_scale),
        grid=(nt,),
        in_specs=[row_spec(d), full(g), full(wn), full(wqt), full(wvt)],
        out_specs=[row_spec(w), row_spec(w), row_spec(w), row_spec(w), t_spec, t_spec],
        out_shape=[
            jax.ShapeDtypeStruct((n, w), BF16),
            jax.ShapeDtypeStruct((n, w), F32),
            jax.ShapeDtypeStruct((n, w), F32),
            jax.ShapeDtypeStruct((n, w), F32),
            jax.ShapeDtypeStruct((nt, w, tm), BF16),
            jax.ShapeDtypeStruct((nt, w, tm), BF16),
        ],
        compiler_params=pltpu.CompilerParams(
            dimension_semantics=("arbitrary",), vmem_limit_bytes=VMEM_LIMIT_BYTES),
        name="inproj",
    )(xf, g, wn, wqt, wvt)


LEFT, DIAG, RIGHT = 0, 1, 2
SIGMA = {LEFT: 1.0, DIAG: 0.0, RIGHT: -1.0}


def _attn_kernel(qt_ref, kk_ref, vt_ref, ga_ref, sg_ref, lq1_ref, lk1_ref, lq2_ref, lk2_ref,
                 o_ref, pos_ref, w_ref, bias_ref, s_ref, acc1_ref, acc2_ref, *, lambda_init):
    n_q = qt_ref.shape[1]
    n_k = vt_ref.shape[1]
    tq, tk = Q_TILE, K_TILE
    head = pl.program_id(1)

    slope_bits = (127 - 2 * (head + 1)) << 23
    slope = lax.bitcast_convert_type(jnp.full((1, 1), slope_bits, jnp.int32), F32)

    lam = (jnp.exp(jnp.sum(lq1_ref[...] * lk1_ref[...], axis=-1, keepdims=True))
           - jnp.exp(jnp.sum(lq2_ref[...] * lk2_ref[...], axis=-1, keepdims=True))
           + lambda_init)

    s_len = pos_ref.shape[0]
    kpos = lax.broadcasted_iota(jnp.int32, (s_len, 128), 0)
    lane = lax.broadcasted_iota(jnp.int32, (s_len, 128), 1)
    hi = ((kpos >> POS_SPLIT_BITS) << POS_SPLIT_BITS).astype(F32)
    lo = (kpos & ((1 << POS_SPLIT_BITS) - 1)).astype(F32)
    pos_ref[...] = jnp.where(lane == 0, hi, jnp.where(lane == 1, lo, 0.0)).astype(BF16)

    di = lax.broadcasted_iota(jnp.int32, (tk, tq), 0) - lax.broadcasted_iota(jnp.int32, (tk, tq), 1)
    bias_ref[...] = -slope * jnp.abs(di).astype(F32)

    ones_rows = jnp.where(lax.broadcasted_iota(jnp.int32, (ONES_ROWS, tk), 0) == 0, 1.0, 0.0).astype(BF16)
    row128 = lax.broadcasted_iota(jnp.int32, (128, tq), 0)
    qlane = lax.broadcasted_iota(jnp.int32, (1, tq), 1)

    neg_big = jnp.full((1, tq), NEG_BIG, F32)
    accs = (acc1_ref, acc2_ref)

    def build_weights(qi):
        qq = qt_ref[0, qi].astype(F32)
        tops = (jnp.where(row128 < HEAD_DIM, qq, 0.0).astype(BF16),
                jnp.where(row128 >= HEAD_DIM, qq, 0.0).astype(BF16))
        for v in (LEFT, DIAG, RIGHT):
            bottom = jnp.where(row128 < 2, SIGMA[v] * slope, 0.0).astype(BF16)
            for t in range(2):
                w_ref[v, t, 0:128, :] = tops[t]
                w_ref[v, t, 128:256, :] = bottom

    def seq(i, qi):
        j = i - 1
        return jnp.where(i == 0, qi, j + jnp.where(j >= qi, 1, 0))

    def shift_of(c, qi):
        side = jnp.where(c < qi, -1.0, jnp.where(c > qi, 1.0, 0.0))
        return side * ((qlane + qi * tq).astype(F32) * slope)

    def scores(c, qi, slot, diag):
        k0 = c * tk if isinstance(c, int) else pl.multiple_of(c * tk, tk)
        lhs = jnp.concatenate([kk_ref[0, pl.ds(k0, tk), :], pos_ref[pl.ds(k0, tk), :]], axis=1)
        v = DIAG if diag else jnp.where(c < qi, LEFT, RIGHT)
        cms = []
        for t in range(2):
            s = jnp.dot(lhs, w_ref[v, t], preferred_element_type=F32)
            if diag:
                s = s + bias_ref[...]
            s_ref[slot, t] = s
            cm = jnp.max(s, axis=0, keepdims=True)
            cms.append(cm if diag else cm + shift_of(c, qi))
        return tuple(cms)

    def softmax_pv(c, qi, slot, m_run, cms):
        vt = jnp.concatenate([vt_ref[0, c], ones_rows], axis=0)
        shift = shift_of(c, qi)
        new = []
        for t in range(2):
            m_new = jnp.maximum(m_run[t], cms[t])
            alpha = jnp.exp(m_run[t] - m_new)
            p = jnp.exp(s_ref[slot, t] - (m_new - shift)).astype(BF16)
            pv = jnp.dot(vt, p, preferred_element_type=F32)
            accs[t][...] = alpha * accs[t][...] + pv
            new.append(m_new)
        return tuple(new)

    def finalize(qi):
        q0 = pl.multiple_of(qi * tq, tq)
        a1 = acc1_ref[...]
        a2 = acc2_ref[...]
        att = a1[0:V_DIM] / a1[V_DIM:V_DIM + 1] - lam * (a2[0:V_DIM] / a2[V_DIM:V_DIM + 1])
        ms = jnp.mean(att * att, axis=0, keepdims=True)
        y = att * lax.rsqrt(ms + EPS) * sg_ref[...]
        o_ref[0, pl.ds(q0, tq), :] = (y.T * ga_ref[0, pl.ds(q0, tq), :]).astype(BF16)

    assert n_k % 2 == 0 and n_k >= 2

    def q_tile_body(qi, cms_diag):
        def pipe(jj, carry):
            m_run, cms = carry
            i = 2 * jj
            cms_odd = scores(seq(i + 1, qi), qi, 1, False)
            m_run = softmax_pv(seq(i, qi), qi, 0, m_run, cms)
            cms_even = scores(seq(i + 2, qi), qi, 0, False)
            m_run = softmax_pv(seq(i + 1, qi), qi, 1, m_run, cms_odd)
            return m_run, cms_even

        m_run, cms = lax.fori_loop(0, (n_k - 2) // 2, pipe, ((neg_big, neg_big), cms_diag))
        cms_last = scores(seq(n_k - 1, qi), qi, 1, False)
        m_run = softmax_pv(seq(n_k - 2, qi), qi, 0, m_run, cms)
        q_next = jnp.minimum(qi + 1, n_q - 1)
        build_weights(q_next)
        cms_diag_next = scores(q_next, q_next, 0, True)
        softmax_pv(seq(n_k - 1, qi), qi, 1, m_run, cms_last)
        finalize(qi)
        return cms_diag_next

    acc1_ref[...] = jnp.zeros_like(acc1_ref)
    acc2_ref[...] = jnp.zeros_like(acc2_ref)
    build_weights(0)
    lax.fori_loop(0, n_q, q_tile_body, scores(0, 0, 0, True))


def _attention(qt, kk, vt, ga, sg, lq1, lk1, lq2, lk2, lambda_init):
    b, n_q, _, tq = qt.shape
    _, n_k, _, tk = vt.shape
    s = kk.shape[1]
    assert tq == Q_TILE and tk == K_TILE and n_q * tq == s and n_k * tk == s
    assert s <= 1 << (POS_SPLIT_BITS + 8)
    full = lambda a: pl.BlockSpec(a.shape, lambda bi, hi: (0,) * a.ndim)
    t_spec = lambda nblk, width: pl.BlockSpec((1, nblk, V_DIM, width), lambda bi, hi: (bi, 0, hi, 0))
    r_spec = pl.BlockSpec((1, s, V_DIM), lambda bi, hi: (bi, 0, hi))
    return pl.pallas_call(
        functools.partial(_attn_kernel, lambda_init=lambda_init),
        grid=(b, N_HEADS),
        in_specs=[t_spec(n_q, tq), r_spec, t_spec(n_k, tk), r_spec,
                  full(sg), full(lq1), full(lk1), full(lq2), full(lk2)],
        out_specs=r_spec,
        out_shape=jax.ShapeDtypeStruct((b, s, ATTN_WIDTH), BF16),
        scratch_shapes=[
            pltpu.VMEM((s, 128), BF16),
            pltpu.VMEM((3, 2, 256, Q_TILE), BF16),
            pltpu.VMEM((K_TILE, Q_TILE), F32),
            pltpu.VMEM((2, 2, K_TILE, Q_TILE), F32),
            pltpu.VMEM((V_DIM + ONES_ROWS, Q_TILE), F32),
            pltpu.VMEM((V_DIM + ONES_ROWS, Q_TILE), F32),
        ],
        compiler_params=pltpu.CompilerParams(
            dimension_semantics=("arbitrary", "arbitrary"), vmem_limit_bytes=VMEM_LIMIT_BYTES),
        name="diff_attn",
    )(qt, kk, vt, ga, sg, lq1, lk1, lq2, lk2)


def _outproj_kernel(x_ref, oa_ref, cg_ref, u_ref, up_ref, un_ref, p_ref, cw_ref, wo_ref,
                    g2_ref, wg_ref, wp_ref, gf_ref, out_ref, *, tiles_per_seq, final_norm):
    i = pl.program_id(0)
    tm = x_ref.shape[0]
    u = u_ref[...]
    row = lax.broadcasted_iota(jnp.int32, (tm, 1), 0)
    t_in_seq = i % tiles_per_seq
    prev_row = jnp.where(t_in_seq == 0, 0.0, up_ref[7:8, :])
    next_row = jnp.where(t_in_seq == tiles_per_seq - 1, 0.0, un_ref[0:1, :])
    u_prev = jnp.where(row == 0, prev_row, pltpu.roll(u, 1, axis=0))
    u_next = jnp.where(row == tm - 1, next_row, pltpu.roll(u, tm - 1, axis=0))
    cw = cw_ref[...]
    conv = cw[0:1] * u_prev + cw[1:2] * u + cw[2:3] * u_next
    o_conv = (cg_ref[...] * conv).astype(BF16)
    aw = oa_ref.shape[1]
    mix = (jnp.dot(oa_ref[...], wo_ref[0:aw, :], preferred_element_type=F32)
           + jnp.dot(o_conv, wo_ref[aw:, :], preferred_element_type=F32))
    x1 = x_ref[...] + mix
    ms1 = jnp.mean(x1 * x1, axis=-1, keepdims=True)
    n1 = (x1 * lax.rsqrt(ms1 + EPS) * g2_ref[...]).astype(BF16)
    gate = jax.nn.sigmoid(jnp.dot(n1, wg_ref[...], preferred_element_type=F32))
    pp = jnp.dot(p_ref[...].astype(BF16), wp_ref[...], preferred_element_type=F32)
    x2 = x1 + gate * pp
    if final_norm:
        ms2 = jnp.mean(x2 * x2, axis=-1, keepdims=True)
        x2 = x2 * lax.rsqrt(ms2 + EPS) * gf_ref[...]
    out_ref[...] = x2


def _outproj(xf, oa, cg, u, pf, cw, wo, g2, wg, wp, gf, seq_len, final_norm):
    n, d = xf.shape
    tm = ROW_TILE
    nt = n // tm
    sub = 8
    nsub = n // sub
    row_spec = lambda width: pl.BlockSpec((tm, width), lambda i: (i, 0))
    full = lambda a: pl.BlockSpec(a.shape, lambda i: (0,) * a.ndim)
    prev_spec = pl.BlockSpec((sub, CONV_WIDTH), lambda i: (jnp.maximum(i * (tm // sub) - 1, 0), 0))
    next_spec = pl.BlockSpec((sub, CONV_WIDTH), lambda i: (jnp.minimum((i + 1) * (tm // sub), nsub - 1), 0))
    return pl.pallas_call(
        functools.partial(_outproj_kernel, tiles_per_seq=seq_len // tm, final_norm=final_norm),
        grid=(nt,),
        in_specs=[row_spec(d), row_spec(ATTN_WIDTH), row_spec(CONV_WIDTH), row_spec(CONV_WIDTH),
                  prev_spec, next_spec, row_spec(pf.shape[1]),
                  full(cw), full(wo), full(g2), full(wg), full(wp), full(gf)],
        out_specs=row_spec(d),
        out_shape=jax.ShapeDtypeStruct((n, d), F32),
        compiler_params=pltpu.CompilerParams(
            dimension_semantics=("arbitrary",), vmem_limit_bytes=VMEM_LIMIT_BYTES),
        name="outproj",
    )(xf, oa, cg, u, u, u, pf, cw, wo, g2, wg, wp, gf)


def kernel(x, p, mix_norm_g, w_in, lambda_q1, lambda_k1, lambda_q2, lambda_k2, subln_g, conv_w,
           w_out, ple_norm_g, w_ple_gate, w_ple_proj, final_norm_g):
    b, s, d = x.shape
    depth = w_in.shape[0]
    assert s % ROW_TILE == 0 and ROW_TILE == Q_TILE == K_TILE
    aw = ATTN_WIDTH
    xf = x.reshape(b * s, d)
    gf = final_norm_g.reshape(1, d)
    for i in range(depth):
        lambda_init = 0.8 - 0.6 * math.exp(-0.3 * i)
        wi = w_in[i]
        wn = jnp.concatenate([wi[:, aw:2 * aw], wi[:, 3 * aw:]], axis=1).astype(BF16)
        wqt = (wi[:, 0:aw] * HEAD_DIM ** -0.5).T.astype(BF16)
        wvt = wi[:, 2 * aw:3 * aw].T.astype(BF16)
        kk, ga, cg, u, qt, vt = _inproj(xf, mix_norm_g[i].reshape(1, d), wn, wqt, wvt,
                                        1.0 - lambda_init)
        n_t = s // ROW_TILE
        oa = _attention(qt.reshape(b, n_t, aw, ROW_TILE), kk.reshape(b, s, aw),
                        vt.reshape(b, n_t, aw, ROW_TILE), ga.reshape(b, s, aw),
                        subln_g[i].reshape(V_DIM, 1),
                        lambda_q1[i].reshape(1, -1), lambda_k1[i].reshape(1, -1),
                        lambda_q2[i].reshape(1, -1), lambda_k2[i].reshape(1, -1), lambda_init)
        xf = _outproj(xf, oa.reshape(b * s, aw), cg, u, p[i].reshape(b * s, -1), conv_w[i],
                      w_out[i].astype(BF16), ple_norm_g[i].reshape(1, d),
                      w_ple_gate[i].astype(BF16), w_ple_proj[i].astype(BF16), gf, s,
                      final_norm=(i == depth - 1))
    return xf.reshape(b, s, d)
```

```python
import functools
import math

import jax
import jax.numpy as jnp
from jax import lax
from jax.experimental import pallas as pl
from jax.experimental.pallas import tpu as pltpu

F32 = jnp.float32
BF16 = jnp.bfloat16

N_HEADS = 4
HEAD_DIM = 64
V_DIM = 2 * HEAD_DIM
ATTN_WIDTH = N_HEADS * V_DIM
CONV_WIDTH = 512
EPS = 1e-6

ROW_TILE = 512
Q_TILE = 512
K_TILE = 512
POS_SPLIT_BITS = 6
ONES_ROWS = 16
NEG_BIG = -1e30
VMEM_LIMIT_BYTES = 48 * 1024 * 1024


def _silu(v):
    return v * jax.nn.sigmoid(v)


def _inproj_kernel(x_ref, g_ref, wn_ref, wqt_ref, wvt_ref,
                   kk_ref, ga_ref, cg_ref, u_ref, qt_ref, vt_ref, *, gate_scale):
    x = x_ref[...]
    ms = jnp.mean(x * x, axis=-1, keepdims=True)
    h = (x * lax.rsqrt(ms + EPS) * g_ref[...]).astype(BF16)
    zn = jnp.dot(h, wn_ref[...], preferred_element_type=F32)
    w = ATTN_WIDTH
    kk_ref[...] = zn[:, 0:w].astype(BF16)
    ga_ref[...] = _silu(zn[:, w:2 * w]) * gate_scale
    cb = zn[:, 2 * w:3 * w]
    cc = zn[:, 3 * w:4 * w]
    ch = zn[:, 4 * w:5 * w]
    gc = zn[:, 5 * w:6 * w]
    cg_ref[...] = cb * _silu(gc)
    u_ref[...] = cc * ch
    nt = (((1,), (1,)), ((), ()))
    qt_ref[0] = lax.dot_general(wqt_ref[...], h, nt, preferred_element_type=F32).astype(BF16)
    vt_ref[0] = lax.dot_general(wvt_ref[...], h, nt, preferred_element_type=F32).astype(BF16)


def _inproj(xf, g, wn, wqt, wvt, gate_scale):
    n, d = xf.shape
    tm = ROW_TILE
    nt = n // tm
    w = ATTN_WIDTH
    row_spec = lambda width: pl.BlockSpec((tm, width), lambda i: (i, 0))
    full = lambda a: pl.BlockSpec(a.shape, lambda i: (0,) * a.ndim)
    t_spec = pl.BlockSpec((1, w, tm), lambda i: (i, 0, 0))
    return pl.pallas_call(
        functools.partial(_inproj_kernel, gate_scale=gate_scale),
        grid=(nt,),
        in_specs=[row_spec(d), full(g), full(wn), full(wqt), full(wvt)],
        out_specs=[row_spec(w), row_spec(w), row_spec(w), row_spec(w), t_spec, t_spec],
        out_shape=[
            jax.ShapeDtypeStruct((n, w), BF16),
            jax.ShapeDtypeStruct((n, w), F32),
            jax.ShapeDtypeStruct((n, w), F32),
            jax.ShapeDtypeStruct((n, w), F32),
            jax.ShapeDtypeStruct((nt, w, tm), BF16),
            jax.ShapeDtypeStruct((nt, w, tm), BF16),
        ],
        compiler_params=pltpu.CompilerParams(
            dimension_semantics=("arbitrary",), vmem_limit_bytes=VMEM_LIMIT_BYTES),
        name="inproj",
    )(xf, g, wn, wqt, wvt)


LEFT, DIAG, RIGHT = 0, 1, 2
SIGMA = {LEFT: 1.0, DIAG: 0.0, RIGHT: -1.0}


def _attn_kernel(qt_ref, kk_ref, vt_ref, ga_ref, sg_ref, lq1_ref, lk1_ref, lq2_ref, lk2_ref,
                 o_ref, pos_ref, w_ref, bias_ref, s_ref, acc1_ref, acc2_ref, *, lambda_init):
    n_q = qt_ref.shape[1]
    n_k = vt_ref.shape[1]
    tq, tk = Q_TILE, K_TILE
    head = pl.program_id(1)

    slope_bits = (127 - 2 * (head + 1)) << 23
    slope = lax.bitcast_convert_type(jnp.full((1, 1), slope_bits, jnp.int32), F32)

    lam = (jnp.exp(jnp.sum(lq1_ref[...] * lk1_ref[...], axis=-1, keepdims=True))
           - jnp.exp(jnp.sum(lq2_ref[...] * lk2_ref[...], axis=-1, keepdims=True))
           + lambda_init)

    s_len = pos_ref.shape[0]
    kpos = lax.broadcasted_iota(jnp.int32, (s_len, 128), 0)
    lane = lax.broadcasted_iota(jnp.int32, (s_len, 128), 1)
    hi = ((kpos >> POS_SPLIT_BITS) << POS_SPLIT_BITS).astype(F32)
    lo = (kpos & ((1 << POS_SPLIT_BITS) - 1)).astype(F32)
    pos_ref[...] = jnp.where(lane == 0, hi, jnp.where(lane == 1, lo, 0.0)).astype(BF16)

    di = lax.broadcasted_iota(jnp.int32, (tk, tq), 0) - lax.broadcasted_iota(jnp.int32, (tk, tq), 1)
    bias_ref[...] = -slope * jnp.abs(di).astype(F32)

    ones_rows = jnp.where(lax.broadcasted_iota(jnp.int32, (ONES_ROWS, tk), 0) == 0, 1.0, 0.0).astype(BF16)
    row128 = lax.broadcasted_iota(jnp.int32, (128, tq), 0)
    qlane = lax.broadcasted_iota(jnp.int32, (1, tq), 1)

    neg_big = jnp.full((1, tq), NEG_BIG, F32)
    accs = (acc1_ref, acc2_ref)

    def build_weights(qi, buf):
        qq = qt_ref[0, qi].astype(F32)
        tops = (jnp.where(row128 < HEAD_DIM, qq, 0.0).astype(BF16),
                jnp.where(row128 >= HEAD_DIM, qq, 0.0).astype(BF16))
        for v in (LEFT, DIAG, RIGHT):
            bottom = jnp.where(row128 < 2, SIGMA[v] * slope, 0.0).astype(BF16)
            for t in range(2):
                w_ref[buf, v, t, 0:128, :] = tops[t]
                w_ref[buf, v, t, 128:256, :] = bottom

    def seq(i, qi):
        j = i - 1
        return jnp.where(i == 0, qi, j + jnp.where(j >= qi, 1, 0))

    def shift_of(c, qi):
        side = jnp.where(c < qi, -1.0, jnp.where(c > qi, 1.0, 0.0))
        return side * ((qlane + qi * tq).astype(F32) * slope)

    def scores(c, qi, buf, slot, diag):
        k0 = c * tk if isinstance(c, int) else pl.multiple_of(c * tk, tk)
        lhs = jnp.concatenate([kk_ref[0, pl.ds(k0, tk), :], pos_ref[pl.ds(k0, tk), :]], axis=1)
        v = DIAG if diag else jnp.where(c < qi, LEFT, RIGHT)
        cms = []
        for t in range(2):
            s = jnp.dot(lhs, w_ref[buf, v, t], preferred_element_type=F32)
            if diag:
                s = s + bias_ref[...]
            s_ref[slot, t] = s
            cm = jnp.max(s, axis=0, keepdims=True)
            cms.append(cm if diag else cm + shift_of(c, qi))
        return tuple(cms)

    def softmax_pv(c, qi, slot, m_run, cms):
        vt = jnp.concatenate([vt_ref[0, c], ones_rows], axis=0)
        shift = shift_of(c, qi)
        new = []
        for t in range(2):
            m_new = jnp.maximum(m_run[t], cms[t])
            alpha = jnp.exp(m_run[t] - m_new)
            p = jnp.exp(s_ref[slot, t] - (m_new - shift)).astype(BF16)
            pv = jnp.dot(vt, p, preferred_element_type=F32)
            accs[t][...] = alpha * accs[t][...] + pv
            new.append(m_new)
        return tuple(new)

    def finalize(qi):
        q0 = pl.multiple_of(qi * tq, tq)
        a1 = acc1_ref[...]
        a2 = acc2_ref[...]
        att = a1[0:V_DIM] / a1[V_DIM:V_DIM + 1] - lam * (a2[0:V_DIM] / a2[V_DIM:V_DIM + 1])
        ms = jnp.mean(att * att, axis=0, keepdims=True)
        y = att * lax.rsqrt(ms + EPS) * sg_ref[...]
        o_ref[0, pl.ds(q0, tq), :] = (y.T * ga_ref[0, pl.ds(q0, tq), :]).astype(BF16)

    assert n_k % 2 == 0 and n_k >= 2

    def q_tile_body(qi, cms_diag):
        buf = qi % 2

        def pipe(jj, carry):
            m_run, cms = carry
            i = 2 * jj
            cms_odd = scores(seq(i + 1, qi), qi, buf, 1, False)
            m_run = softmax_pv(seq(i, qi), qi, 0, m_run, cms)
            cms_even = scores(seq(i + 2, qi), qi, buf, 0, False)
            m_run = softmax_pv(seq(i + 1, qi), qi, 1, m_run, cms_odd)
            return m_run, cms_even

        m_run, cms = lax.fori_loop(0, (n_k - 2) // 2, pipe, ((neg_big, neg_big), cms_diag))
        cms_last = scores(seq(n_k - 1, qi), qi, buf, 1, False)
        m_run = softmax_pv(seq(n_k - 2, qi), qi, 0, m_run, cms)
        q_next = jnp.minimum(qi + 1, n_q - 1)
        build_weights(q_next, 1 - buf)
        cms_diag_next = scores(q_next, q_next, 1 - buf, 0, True)
        softmax_pv(seq(n_k - 1, qi), qi, 1, m_run, cms_last)
        finalize(qi)
        return cms_diag_next

    acc1_ref[...] = jnp.zeros_like(acc1_ref)
    acc2_ref[...] = jnp.zeros_like(acc2_ref)
    build_weights(0, 0)
    lax.fori_loop(0, n_q, q_tile_body, scores(0, 0, 0, 0, True))


def _attention(qt, kk, vt, ga, sg, lq1, lk1, lq2, lk2, lambda_init):
    b, n_q, _, tq = qt.shape
    _, n_k, _, tk = vt.shape
    s = kk.shape[1]
    assert tq == Q_TILE and tk == K_TILE and n_q * tq == s and n_k * tk == s
    assert s <= 1 << (POS_SPLIT_BITS + 8)
    full = lambda a: pl.BlockSpec(a.shape, lambda bi, hi: (0,) * a.ndim)
    t_spec = lambda nblk, width: pl.BlockSpec((1, nblk, V_DIM, width), lambda bi, hi: (bi, 0, hi, 0))
    r_spec = pl.BlockSpec((1, s, V_DIM), lambda bi, hi: (bi, 0, hi))
    return pl.pallas_call(
        functools.partial(_attn_kernel, lambda_init=lambda_init),
        grid=(b, N_HEADS),
        in_specs=[t_spec(n_q, tq), r_spec, t_spec(n_k, tk), r_spec,
                  full(sg), full(lq1), full(lk1), full(lq2), full(lk2)],
        out_specs=r_spec,
        out_shape=jax.ShapeDtypeStruct((b, s, ATTN_WIDTH), BF16),
        scratch_shapes=[
            pltpu.VMEM((s, 128), BF16),
            pltpu.VMEM((2, 3, 2, 256, Q_TILE), BF16),
            pltpu.VMEM((K_TILE, Q_TILE), F32),
            pltpu.VMEM((2, 2, K_TILE, Q_TILE), F32),
            pltpu.VMEM((V_DIM + ONES_ROWS, Q_TILE), F32),
            pltpu.VMEM((V_DIM + ONES_ROWS, Q_TILE), F32),
        ],
        compiler_params=pltpu.CompilerParams(
            dimension_semantics=("arbitrary", "arbitrary"), vmem_limit_bytes=VMEM_LIMIT_BYTES),
        name="diff_attn",
    )(qt, kk, vt, ga, sg, lq1, lk1, lq2, lk2)


def _outproj_kernel(x_ref, oa_ref, cg_ref, u_ref, up_ref, un_ref, p_ref, cw_ref, wo_ref,
                    g2_ref, wg_ref, wp_ref, gf_ref, out_ref, *, tiles_per_seq, final_norm):
    i = pl.program_id(0)
    tm = x_ref.shape[0]
    u = u_ref[...]
    row = lax.broadcasted_iota(jnp.int32, (tm, 1), 0)
    t_in_seq = i % tiles_per_seq
    prev_row = jnp.where(t_in_seq == 0, 0.0, up_ref[7:8, :])
    next_row = jnp.where(t_in_seq == tiles_per_seq - 1, 0.0, un_ref[0:1, :])
    u_prev = jnp.where(row == 0, prev_row, pltpu.roll(u, 1, axis=0))
    u_next = jnp.where(row == tm - 1, next_row, pltpu.roll(u, tm - 1, axis=0))
    cw = cw_ref[...]
    conv = cw[0:1] * u_prev + cw[1:2] * u + cw[2:3] * u_next
    o_conv = (cg_ref[...] * conv).astype(BF16)
    aw = oa_ref.shape[1]
    mix = (jnp.dot(oa_ref[...], wo_ref[0:aw, :], preferred_element_type=F32)
           + jnp.dot(o_conv, wo_ref[aw:, :], preferred_element_type=F32))
    x1 = x_ref[...] + mix
    ms1 = jnp.mean(x1 * x1, axis=-1, keepdims=True)
    n1 = (x1 * lax.rsqrt(ms1 + EPS) * g2_ref[...]).astype(BF16)
    gate = jax.nn.sigmoid(jnp.dot(n1, wg_ref[...], preferred_element_type=F32))
    pp = jnp.dot(p_ref[...].astype(BF16), wp_ref[...], preferred_element_type=F32)
    x2 = x1 + gate * pp
    if final_norm:
        ms2 = jnp.mean(x2 * x2, axis=-1, keepdims=True)
        x2 = x2 * lax.rsqrt(ms2 + EPS) * gf_ref[...]
    out_ref[...] = x2


def _outproj(xf, oa, cg, u, pf, cw, wo, g2, wg, wp, gf, seq_len, final_norm):
    n, d = xf.shape
    tm = ROW_TILE
    nt = n // tm
    sub = 8
    nsub = n // sub
    row_spec = lambda width: pl.BlockSpec((tm, width), lambda i: (i, 0))
    full = lambda a: pl.BlockSpec(a.shape, lambda i: (0,) * a.ndim)
    prev_spec = pl.BlockSpec((sub, CONV_WIDTH), lambda i: (jnp.maximum(i * (tm // sub) - 1, 0), 0))
    next_spec = pl.BlockSpec((sub, CONV_WIDTH), lambda i: (jnp.minimum((i + 1) * (tm // sub), nsub - 1), 0))
    return pl.pallas_call(
        functools.partial(_outproj_kernel, tiles_per_seq=seq_len // tm, final_norm=final_norm),
        grid=(nt,),
        in_specs=[row_spec(d), row_spec(ATTN_WIDTH), row_spec(CONV_WIDTH), row_spec(CONV_WIDTH),
                  prev_spec, next_spec, row_spec(pf.shape[1]),
                  full(cw), full(wo), full(g2), full(wg), full(wp), full(gf)],
        out_specs=row_spec(d),
        out_shape=jax.ShapeDtypeStruct((n, d), F32),
        compiler_params=pltpu.CompilerParams(
            dimension_semantics=("arbitrary",), vmem_limit_bytes=VMEM_LIMIT_BYTES),
        name="outproj",
    )(xf, oa, cg, u, u, u, pf, cw, wo, g2, wg, wp, gf)


def kernel(x, p, mix_norm_g, w_in, lambda_q1, lambda_k1, lambda_q2, lambda_k2, subln_g, conv_w,
           w_out, ple_norm_g, w_ple_gate, w_ple_proj, final_norm_g):
    b, s, d = x.shape
    depth = w_in.shape[0]
    assert s % ROW_TILE == 0 and ROW_TILE == Q_TILE == K_TILE
    aw = ATTN_WIDTH
    xf = x.reshape(b * s, d)
    gf = final_norm_g.reshape(1, d)
    for i in range(depth):
        lambda_init = 0.8 - 0.6 * math.exp(-0.3 * i)
        wi = w_in[i]
        wn = jnp.concatenate([wi[:, aw:2 * aw], wi[:, 3 * aw:]], axis=1).astype(BF16)
        wqt = (wi[:, 0:aw] * HEAD_DIM ** -0.5).T.astype(BF16)
        wvt = wi[:, 2 * aw:3 * aw].T.astype(BF16)
        kk, ga, cg, u, qt, vt = _inproj(xf, mix_norm_g[i].reshape(1, d), wn, wqt, wvt,
                                        1.0 - lambda_init)
        n_t = s // ROW_TILE
        oa = _attention(qt.reshape(b, n_t, aw, ROW_TILE), kk.reshape(b, s, aw),
                        vt.reshape(b, n_t, aw, ROW_TILE), ga.reshape(b, s, aw),
                        subln_g[i].reshape(V_DIM, 1),
                        lambda_q1[i].reshape(1, -1), lambda_k1[i].reshape(1, -1),
                        lambda_q2[i].reshape(1, -1), lambda_k2[i].reshape(1, -1), lambda_init)
        xf = _outproj(xf, oa.reshape(b * s, aw), cg, u, p[i].reshape(b * s, -1), conv_w[i],
                      w_out[i].astype(BF16), ple_norm_g[i].reshape(1, d),
                      w_ple_gate[i].astype(BF16), w_ple_proj[i].astype(BF16), gf, s,
                      final_norm=(i == depth - 1))
    return xf.reshape(b, s, d)
```

```python
import functools
import math

import jax
import jax.numpy as jnp
from jax import lax
from jax.experimental import pallas as pl
from jax.experimental.pallas import tpu as pltpu

F32 = jnp.float32
BF16 = jnp.bfloat16

N_HEADS = 4
HEAD_DIM = 64
V_DIM = 2 * HEAD_DIM
ATTN_WIDTH = N_HEADS * V_DIM
CONV_WIDTH = 512
EPS = 1e-6

ROW_TILE = 512
Q_TILE = 512
K_TILE = 512
POS_SPLIT_BITS = 6
ONES_ROWS = 16
NEG_BIG = -1e30
VMEM_LIMIT_BYTES = 48 * 1024 * 1024


def _silu(v):
    return v * jax.nn.sigmoid(v)


def _inproj_kernel(x_ref, g_ref, wn_ref, wqt_ref, wvt_ref,
                   kk_ref, ga_ref, cg_ref, u_ref, qt_ref, vt_ref, *, gate_scale):
    x = x_ref[...]
    ms = jnp.mean(x * x, axis=-1, keepdims=True)
    h = (x * lax.rsqrt(ms + EPS) * g_ref[...]).astype(BF16)
    zn = jnp.dot(h, wn_ref[...], preferred_element_type=F32)
    w = ATTN_WIDTH
    kk_ref[...] = zn[:, 0:w].astype(BF16)
    ga_ref[...] = _silu(zn[:, w:2 * w]) * gate_scale
    cb = zn[:, 2 * w:3 * w]
    cc = zn[:, 3 * w:4 * w]
    ch = zn[:, 4 * w:5 * w]
    gc = zn[:, 5 * w:6 * w]
    cg_ref[...] = cb * _silu(gc)
    u_ref[...] = cc * ch
    nt = (((1,), (1,)), ((), ()))
    qt_ref[0] = lax.dot_general(wqt_ref[...], h, nt, preferred_element_type=F32).astype(BF16)
    vt_ref[0] = lax.dot_general(wvt_ref[...], h, nt, preferred_element_type=F32).astype(BF16)


def _inproj(xf, g, wn, wqt, wvt, gate_scale):
    n, d = xf.shape
    tm = ROW_TILE
    nt = n // tm
    w = ATTN_WIDTH
    row_spec = lambda width: pl.BlockSpec((tm, width), lambda i: (i, 0))
    full = lambda a: pl.BlockSpec(a.shape, lambda i: (0,) * a.ndim)
    t_spec = pl.BlockSpec((1, w, tm), lambda i: (i, 0, 0))
    return pl.pallas_call(
        functools.partial(_inproj_kernel, gate_scale=gate_scale),
        grid=(nt,),
        in_specs=[row_spec(d), full(g), full(wn), full(wqt), full(wvt)],
        out_specs=[row_spec(w), row_spec(w), row_spec(w), row_spec(w), t_spec, t_spec],
        out_shape=[
            jax.ShapeDtypeStruct((n, w), BF16),
            jax.ShapeDtypeStruct((n, w), F32),
            jax.ShapeDtypeStruct((n, w), F32),
            jax.ShapeDtypeStruct((n, w), F32),
            jax.ShapeDtypeStruct((nt, w, tm), BF16),
            jax.ShapeDtypeStruct((nt, w, tm), BF16),
        ],
        compiler_params=pltpu.CompilerParams(
            dimension_semantics=("arbitrary",), vmem_limit_bytes=VMEM_LIMIT_BYTES),
        name="inproj",
    )(xf, g, wn, wqt, wvt)


LEFT, DIAG, RIGHT = 0, 1, 2
SIGMA = {LEFT: 1.0, DIAG: 0.0, RIGHT: -1.0}


def _attn_kernel(qt_ref, kk_ref, vt_ref, ga_ref, sg_ref, lq1_ref, lk1_ref, lq2_ref, lk2_ref,
                 o_ref, pos_ref, w_ref, bias_ref, s_ref, acc1_ref, acc2_ref, *, lambda_init):
    n_q = qt_ref.shape[1]
    n_k = vt_ref.shape[1]
    tq, tk = Q_TILE, K_TILE
    head = pl.program_id(1)

    slope_bits = (127 - 2 * (head + 1)) << 23
    slope = lax.bitcast_convert_type(jnp.full((1, 1), slope_bits, jnp.int32), F32)

    lam = (jnp.exp(jnp.sum(lq1_ref[...] * lk1_ref[...], axis=-1, keepdims=True))
           - jnp.exp(jnp.sum(lq2_ref[...] * lk2_ref[...], axis=-1, keepdims=True))
           + lambda_init)

    s_len = pos_ref.shape[0]
    kpos = lax.broadcasted_iota(jnp.int32, (s_len, 128), 0)
    lane = lax.broadcasted_iota(jnp.int32, (s_len, 128), 1)
    hi = ((kpos >> POS_SPLIT_BITS) << POS_SPLIT_BITS).astype(F32)
    lo = (kpos & ((1 << POS_SPLIT_BITS) - 1)).astype(F32)
    pos_ref[...] = jnp.where(lane == 0, hi, jnp.where(lane == 1, lo, 0.0)).astype(BF16)

    di = lax.broadcasted_iota(jnp.int32, (tk, tq), 0) - lax.broadcasted_iota(jnp.int32, (tk, tq), 1)
    bias_ref[...] = -slope * jnp.abs(di).astype(F32)

    ones_rows = jnp.where(lax.broadcasted_iota(jnp.int32, (ONES_ROWS, tk), 0) == 0, 1.0, 0.0).astype(BF16)
    row128 = lax.broadcasted_iota(jnp.int32, (128, tq), 0)
    qlane = lax.broadcasted_iota(jnp.int32, (1, tq), 1)

    neg_big = jnp.full((1, tq), NEG_BIG, F32)
    accs = (acc1_ref, acc2_ref)

    def build_weights(qi, buf):
        qq = qt_ref[0, qi].astype(F32)
        tops = (jnp.where(row128 < HEAD_DIM, qq, 0.0).astype(BF16),
                jnp.where(row128 >= HEAD_DIM, qq, 0.0).astype(BF16))
        for v in (LEFT, DIAG, RIGHT):
            bottom = jnp.where(row128 < 2, SIGMA[v] * slope, 0.0).astype(BF16)
            for t in range(2):
                w_ref[buf, v, t, 0:128, :] = tops[t]
                w_ref[buf, v, t, 128:256, :] = bottom

    def seq(i, qi):
        j = i - 1
        return jnp.where(i == 0, qi, j + jnp.where(j >= qi, 1, 0))

    def shift_of(c, qi):
        side = jnp.where(c < qi, -1.0, jnp.where(c > qi, 1.0, 0.0))
        return side * ((qlane + qi * tq).astype(F32) * slope)

    def scores(c, qi, buf, slot, diag):
        k0 = c * tk if isinstance(c, int) else pl.multiple_of(c * tk, tk)
        lhs = jnp.concatenate([kk_ref[0, pl.ds(k0, tk), :], pos_ref[pl.ds(k0, tk), :]], axis=1)
        v = DIAG if diag else jnp.where(c < qi, LEFT, RIGHT)
        cms = []
        for t in range(2):
            s = jnp.dot(lhs, w_ref[buf, v, t], preferred_element_type=F32)
            if diag:
                s = s + bias_ref[...]
            s_ref[slot, t] = s
            cm = jnp.max(s, axis=0, keepdims=True)
            cms.append(cm if diag else cm + shift_of(c, qi))
        return tuple(cms)

    def softmax_pv(c, qi, slot, m_run, cms):
        vt = jnp.concatenate([vt_ref[0, c], ones_rows], axis=0)
        shift = shift_of(c, qi)
        new = []
        for t in range(2):
            m_new = jnp.maximum(m_run[t], cms[t])
            alpha = jnp.exp(m_run[t] - m_new)
            p = jnp.exp(s_ref[slot, t] - (m_new - shift)).astype(BF16)
            pv = jnp.dot(vt, p, preferred_element_type=F32)
            accs[t][...] = alpha * accs[t][...] + pv
            new.append(m_new)
        return tuple(new)

    def finalize(qi):
        q0 = pl.multiple_of(qi * tq, tq)
        a1 = acc1_ref[...]
        a2 = acc2_ref[...]
        att = a1[0:V_DIM] / a1[V_DIM:V_DIM + 1] - lam * (a2[0:V_DIM] / a2[V_DIM:V_DIM + 1])
        ms = jnp.mean(att * att, axis=0, keepdims=True)
        y = att * lax.rsqrt(ms + EPS) * sg_ref[...]
        o_ref[0, pl.ds(q0, tq), :] = (y.T * ga_ref[0, pl.ds(q0, tq), :]).astype(BF16)

    assert n_k % 2 == 0 and n_k >= 2

    def q_tile_body(qi, cms_diag):
        buf = qi % 2

        def pipe(jj, carry):
            m_run, cms = carry
            i = 2 * jj
            cms_odd = scores(seq(i + 1, qi), qi, buf, 1, False)
            m_run = softmax_pv(seq(i, qi), qi, 0, m_run, cms)
            cms_even = scores(seq(i + 2, qi), qi, buf, 0, False)
            m_run = softmax_pv(seq(i + 1, qi), qi, 1, m_run, cms_odd)
            return m_run, cms_even

        m_run, cms = lax.fori_loop(0, (n_k - 2) // 2, pipe, ((neg_big, neg_big), cms_diag), unroll=True)
        cms_last = scores(seq(n_k - 1, qi), qi, buf, 1, False)
        m_run = softmax_pv(seq(n_k - 2, qi), qi, 0, m_run, cms)
        q_next = jnp.minimum(qi + 1, n_q - 1)
        build_weights(q_next, 1 - buf)
        cms_diag_next = scores(q_next, q_next, 1 - buf, 0, True)
        softmax_pv(seq(n_k - 1, qi), qi, 1, m_run, cms_last)
        finalize(qi)
        return cms_diag_next

    acc1_ref[...] = jnp.zeros_like(acc1_ref)
    acc2_ref[...] = jnp.zeros_like(acc2_ref)
    build_weights(0, 0)
    lax.fori_loop(0, n_q, q_tile_body, scores(0, 0, 0, 0, True))


def _attention(qt, kk, vt, ga, sg, lq1, lk1, lq2, lk2, lambda_init):
    b, n_q, _, tq = qt.shape
    _, n_k, _, tk = vt.shape
    s = kk.shape[1]
    assert tq == Q_TILE and tk == K_TILE and n_q * tq == s and n_k * tk == s
    assert s <= 1 << (POS_SPLIT_BITS + 8)
    full = lambda a: pl.BlockSpec(a.shape, lambda bi, hi: (0,) * a.ndim)
    t_spec = lambda nblk, width: pl.BlockSpec((1, nblk, V_DIM, width), lambda bi, hi: (bi, 0, hi, 0))
    r_spec = pl.BlockSpec((1, s, V_DIM), lambda bi, hi: (bi, 0, hi))
    return pl.pallas_call(
        functools.partial(_attn_kernel, lambda_init=lambda_init),
        grid=(b, N_HEADS),
        in_specs=[t_spec(n_q, tq), r_spec, t_spec(n_k, tk), r_spec,
                  full(sg), full(lq1), full(lk1), full(lq2), full(lk2)],
        out_specs=r_spec,
        out_shape=jax.ShapeDtypeStruct((b, s, ATTN_WIDTH), BF16),
        scratch_shapes=[
            pltpu.VMEM((s, 128), BF16),
            pltpu.VMEM((2, 3, 2, 256, Q_TILE), BF16),
            pltpu.VMEM((K_TILE, Q_TILE), F32),
            pltpu.VMEM((2, 2, K_TILE, Q_TILE), F32),
            pltpu.VMEM((V_DIM + ONES_ROWS, Q_TILE), F32),
            pltpu.VMEM((V_DIM + ONES_ROWS, Q_TILE), F32),
        ],
        compiler_params=pltpu.CompilerParams(
            dimension_semantics=("arbitrary", "arbitrary"), vmem_limit_bytes=VMEM_LIMIT_BYTES),
        name="diff_attn",
    )(qt, kk, vt, ga, sg, lq1, lk1, lq2, lk2)


def _outproj_kernel(x_ref, oa_ref, cg_ref, u_ref, up_ref, un_ref, p_ref, cw_ref, wo_ref,
                    g2_ref, wg_ref, wp_ref, gf_ref, out_ref, *, tiles_per_seq, final_norm):
    i = pl.program_id(0)
    tm = x_ref.shape[0]
    u = u_ref[...]
    row = lax.broadcasted_iota(jnp.int32, (tm, 1), 0)
    t_in_seq = i % tiles_per_seq
    prev_row = jnp.where(t_in_seq == 0, 0.0, up_ref[7:8, :])
    next_row = jnp.where(t_in_seq == tiles_per_seq - 1, 0.0, un_ref[0:1, :])
    u_prev = jnp.where(row == 0, prev_row, pltpu.roll(u, 1, axis=0))
    u_next = jnp.where(row == tm - 1, next_row, pltpu.roll(u, tm - 1, axis=0))
    cw = cw_ref[...]
    conv = cw[0:1] * u_prev + cw[1:2] * u + cw[2:3] * u_next
    o_conv = (cg_ref[...] * conv).astype(BF16)
    aw = oa_ref.shape[1]
    mix = (jnp.dot(oa_ref[...], wo_ref[0:aw, :], preferred_element_type=F32)
           + jnp.dot(o_conv, wo_ref[aw:, :], preferred_element_type=F32))
    x1 = x_ref[...] + mix
    ms1 = jnp.mean(x1 * x1, axis=-1, keepdims=True)
    n1 = (x1 * lax.rsqrt(ms1 + EPS) * g2_ref[...]).astype(BF16)
    gate = jax.nn.sigmoid(jnp.dot(n1, wg_ref[...], preferred_element_type=F32))
    pp = jnp.dot(p_ref[...].astype(BF16), wp_ref[...], preferred_element_type=F32)
    x2 = x1 + gate * pp
    if final_norm:
        ms2 = jnp.mean(x2 * x2, axis=-1, keepdims=True)
        x2 = x2 * lax.rsqrt(ms2 + EPS) * gf_ref[...]
    out_ref[...] = x2


def _outproj(xf, oa, cg, u, pf, cw, wo, g2, wg, wp, gf, seq_len, final_norm):
    n, d = xf.shape
    tm = ROW_TILE
    nt = n // tm
    sub = 8
    nsub = n // sub
    row_spec = lambda width: pl.BlockSpec((tm, width), lambda i: (i, 0))
    full = lambda a: pl.BlockSpec(a.shape, lambda i: (0,) * a.ndim)
    prev_spec = pl.BlockSpec((sub, CONV_WIDTH), lambda i: (jnp.maximum(i * (tm // sub) - 1, 0), 0))
    next_spec = pl.BlockSpec((sub, CONV_WIDTH), lambda i: (jnp.minimum((i + 1) * (tm // sub), nsub - 1), 0))
    return pl.pallas_call(
        functools.partial(_outproj_kernel, tiles_per_seq=seq_len // tm, final_norm=final_norm),
        grid=(nt,),
        in_specs=[row_spec(d), row_spec(ATTN_WIDTH), row_spec(CONV_WIDTH), row_spec(CONV_WIDTH),
                  prev_spec, next_spec, row_spec(pf.shape[1]),
                  full(cw), full(wo), full(g2), full(wg), full(wp), full(gf)],
        out_specs=row_spec(d),
        out_shape=jax.ShapeDtypeStruct((n, d), F32),
        compiler_params=pltpu.CompilerParams(
            dimension_semantics=("arbitrary",), vmem_limit_bytes=VMEM_LIMIT_BYTES),
        name="outproj",
    )(xf, oa, cg, u, u, u, pf, cw, wo, g2, wg, wp, gf)


def kernel(x, p, mix_norm_g, w_in, lambda_q1, lambda_k1, lambda_q2, lambda_k2, subln_g, conv_w,
           w_out, ple_norm_g, w_ple_gate, w_ple_proj, final_norm_g):
    b, s, d = x.shape
    depth = w_in.shape[0]
    assert s % ROW_TILE == 0 and ROW_TILE == Q_TILE == K_TILE
    aw = ATTN_WIDTH
    xf = x.reshape(b * s, d)
    gf = final_norm_g.reshape(1, d)
    for i in range(depth):
        lambda_init = 0.8 - 0.6 * math.exp(-0.3 * i)
        wi = w_in[i]
        wn = jnp.concatenate([wi[:, aw:2 * aw], wi[:, 3 * aw:]], axis=1).astype(BF16)
        wqt = (wi[:, 0:aw] * HEAD_DIM ** -0.5).T.astype(BF16)
        wvt = wi[:, 2 * aw:3 * aw].T.astype(BF16)
        kk, ga, cg, u, qt, vt = _inproj(xf, mix_norm_g[i].reshape(1, d), wn, wqt, wvt,
                                        1.0 - lambda_init)
        n_t = s // ROW_TILE
        oa = _attention(qt.reshape(b, n_t, aw, ROW_TILE), kk.reshape(b, s, aw),
                        vt.reshape(b, n_t, aw, ROW_TILE), ga.reshape(b, s, aw),
                        subln_g[i].reshape(V_DIM, 1),
                        lambda_q1[i].reshape(1, -1), lambda_k1[i].reshape(1, -1),
                        lambda_q2[i].reshape(1, -1), lambda_k2[i].reshape(1, -1), lambda_init)
        xf = _outproj(xf, oa.reshape(b * s, aw), cg, u, p[i].reshape(b * s, -1), conv_w[i],
                      w_out[i].astype(BF16), ple_norm_g[i].reshape(1, d),
                      w_ple_gate[i].astype(BF16), w_ple_proj[i].astype(BF16), gf, s,
                      final_norm=(i == depth - 1))
    return xf.reshape(b, s, d)
```

```python
import functools
import math

import jax
import jax.numpy as jnp
from jax import lax
from jax.experimental import pallas as pl
from jax.experimental.pallas import tpu as pltpu

F32 = jnp.float32
BF16 = jnp.bfloat16

N_HEADS = 4
HEAD_DIM = 64
V_DIM = 2 * HEAD_DIM
ATTN_WIDTH = N_HEADS * V_DIM
CONV_WIDTH = 512
EPS = 1e-6

ROW_TILE = 512
Q_TILE = 512
K_TILE = 512
POS_SPLIT_BITS = 6
SLOPE_PIECES = 3
LOG2E = 1.4426950408889634
ONES_ROWS = 16
NEG_BIG = -1e30
VMEM_LIMIT_BYTES = 48 * 1024 * 1024


def _silu(v):
    return v * jax.nn.sigmoid(v)


def _inproj_kernel(x_ref, g_ref, wn_ref, wqt_ref, wvt_ref,
                   kk_ref, ga_ref, cg_ref, u_ref, qt_ref, vt_ref, *, gate_scale):
    x = x_ref[...]
    ms = jnp.mean(x * x, axis=-1, keepdims=True)
    h = (x * lax.rsqrt(ms + EPS) * g_ref[...]).astype(BF16)
    zn = jnp.dot(h, wn_ref[...], preferred_element_type=F32)
    w = ATTN_WIDTH
    kk_ref[...] = zn[:, 0:w].astype(BF16)
    ga_ref[...] = _silu(zn[:, w:2 * w]) * gate_scale
    cb = zn[:, 2 * w:3 * w]
    cc = zn[:, 3 * w:4 * w]
    ch = zn[:, 4 * w:5 * w]
    gc = zn[:, 5 * w:6 * w]
    cg_ref[...] = cb * _silu(gc)
    u_ref[...] = cc * ch
    nt = (((1,), (1,)), ((), ()))
    qt_ref[0] = lax.dot_general(wqt_ref[...], h, nt, preferred_element_type=F32).astype(BF16)
    vt_ref[0] = lax.dot_general(wvt_ref[...], h, nt, preferred_element_type=F32).astype(BF16)


def _inproj(xf, g, wn, wqt, wvt, gate_scale):
    n, d = xf.shape
    tm = ROW_TILE
    nt = n // tm
    w = ATTN_WIDTH
    row_spec = lambda width: pl.BlockSpec((tm, width), lambda i: (i, 0))
    full = lambda a: pl.BlockSpec(a.shape, lambda i: (0,) * a.ndim)
    t_spec = pl.BlockSpec((1, w, tm), lambda i: (i, 0, 0))
    return pl.pallas_call(
        functools.partial(_inproj_kernel, gate_scale=gate_scale),
        grid=(nt,),
        in_specs=[row_spec(d), full(g), full(wn), full(wqt), full(wvt)],
        out_specs=[row_spec(w), row_spec(w), row_spec(w), row_spec(w), t_spec, t_spec],
        out_shape=[
            jax.ShapeDtypeStruct((n, w), BF16),
            jax.ShapeDtypeStruct((n, w), F32),
            jax.ShapeDtypeStruct((n, w), F32),
            jax.ShapeDtypeStruct((n, w), F32),
            jax.ShapeDtypeStruct((nt, w, tm), BF16),
            jax.ShapeDtypeStruct((nt, w, tm), BF16),
        ],
        compiler_params=pltpu.CompilerParams(
            dimension_semantics=("arbitrary",), vmem_limit_bytes=VMEM_LIMIT_BYTES),
        name="inproj",
    )(xf, g, wn, wqt, wvt)


LEFT, DIAG, RIGHT = 0, 1, 2
SIGMA = {LEFT: 1.0, DIAG: 0.0, RIGHT: -1.0}


def _attn_kernel(qt_ref, kk_ref, vt_ref, ga_ref, sg_ref, lq1_ref, lk1_ref, lq2_ref, lk2_ref,
                 o_ref, pos_ref, w_ref, bias_ref, s_ref, acc1_ref, acc2_ref, *, lambda_init):
    n_q = qt_ref.shape[1]
    n_k = vt_ref.shape[1]
    tq, tk = Q_TILE, K_TILE
    head = pl.program_id(1)

    slope_bits = (127 - 2 * (head + 1)) << 23
    slope = lax.bitcast_convert_type(jnp.full((1, 1), slope_bits, jnp.int32), F32)

    lam = (jnp.exp(jnp.sum(lq1_ref[...] * lk1_ref[...], axis=-1, keepdims=True))
           - jnp.exp(jnp.sum(lq2_ref[...] * lk2_ref[...], axis=-1, keepdims=True))
           + lambda_init)

    slope2 = slope * LOG2E

    s_len = pos_ref.shape[0]
    kpos = lax.broadcasted_iota(jnp.int32, (s_len, 128), 0)
    lane = lax.broadcasted_iota(jnp.int32, (s_len, 128), 1)
    hi = ((kpos >> POS_SPLIT_BITS) << POS_SPLIT_BITS).astype(F32)
    lo = (kpos & ((1 << POS_SPLIT_BITS) - 1)).astype(F32)
    pos_ref[...] = jnp.where(lane < SLOPE_PIECES, hi,
                             jnp.where(lane < 2 * SLOPE_PIECES, lo, 0.0)).astype(BF16)

    di = lax.broadcasted_iota(jnp.int32, (tk, tq), 0) - lax.broadcasted_iota(jnp.int32, (tk, tq), 1)
    bias_ref[...] = -slope2 * jnp.abs(di).astype(F32)

    ones_rows = jnp.where(lax.broadcasted_iota(jnp.int32, (ONES_ROWS, tk), 0) == 0, 1.0, 0.0).astype(BF16)
    row128 = lax.broadcasted_iota(jnp.int32, (128, tq), 0)
    qlane = lax.broadcasted_iota(jnp.int32, (1, tq), 1)

    neg_big = jnp.full((1, tq), NEG_BIG, F32)
    accs = (acc1_ref, acc2_ref)

    rest = jnp.broadcast_to(slope2, (128, tq))
    pieces = []
    for _ in range(SLOPE_PIECES):
        piece = rest.astype(BF16).astype(F32)
        pieces.append(piece)
        rest = rest - piece
    piece_rows = pieces[SLOPE_PIECES - 1]
    for i in range(SLOPE_PIECES - 1):
        piece_rows = jnp.where((row128 == i) | (row128 == i + SLOPE_PIECES), pieces[i], piece_rows)
    for v in (LEFT, DIAG, RIGHT):
        bottom = jnp.where(row128 < 2 * SLOPE_PIECES, SIGMA[v] * piece_rows, 0.0).astype(BF16)
        for buf in range(2):
            for t in range(2):
                w_ref[buf, v, t, 128:256, :] = bottom

    def build_weights(qi, buf):
        qq = qt_ref[0, qi].astype(F32)
        tops = (jnp.where(row128 < HEAD_DIM, qq, 0.0).astype(BF16),
                jnp.where(row128 >= HEAD_DIM, qq, 0.0).astype(BF16))
        for v in (LEFT, DIAG, RIGHT):
            for t in range(2):
                w_ref[buf, v, t, 0:128, :] = tops[t]

    def seq(i, qi):
        j = i - 1
        return jnp.where(i == 0, qi, j + jnp.where(j >= qi, 1, 0))

    def shift_of(c, qi):
        side = jnp.where(c < qi, -1.0, jnp.where(c > qi, 1.0, 0.0))
        return side * ((qlane + qi * tq).astype(F32) * slope2)

    def scores(c, qi, buf, slot, diag):
        k0 = c * tk if isinstance(c, int) else pl.multiple_of(c * tk, tk)
        lhs = jnp.concatenate([kk_ref[0, pl.ds(k0, tk), :], pos_ref[pl.ds(k0, tk), :]], axis=1)
        v = DIAG if diag else jnp.where(c < qi, LEFT, RIGHT)
        cms = []
        for t in range(2):
            s = jnp.dot(lhs, w_ref[buf, v, t], preferred_element_type=F32)
            if diag:
                s = s + bias_ref[...]
            s_ref[slot, t] = s
            cm = jnp.max(s, axis=0, keepdims=True)
            cms.append(cm if diag else cm + shift_of(c, qi))
        return tuple(cms)

    def softmax_pv(c, qi, slot, m_run, cms):
        vt = jnp.concatenate([vt_ref[0, c], ones_rows], axis=0)
        shift = shift_of(c, qi)
        new = []
        for t in range(2):
            m_new = jnp.maximum(m_run[t], cms[t])
            alpha = jnp.exp2(m_run[t] - m_new)
            p = jnp.exp2(s_ref[slot, t] - (m_new - shift)).astype(BF16)
            pv = jnp.dot(vt, p, preferred_element_type=F32)
            accs[t][...] = alpha * accs[t][...] + pv
            new.append(m_new)
        return tuple(new)

    def finalize(qi):
        q0 = pl.multiple_of(qi * tq, tq)
        a1 = acc1_ref[...]
        a2 = acc2_ref[...]
        att = a1[0:V_DIM] / a1[V_DIM:V_DIM + 1] - lam * (a2[0:V_DIM] / a2[V_DIM:V_DIM + 1])
        ms = jnp.mean(att * att, axis=0, keepdims=True)
        y = att * lax.rsqrt(ms + EPS) * sg_ref[...]
        o_ref[0, pl.ds(q0, tq), :] = (y.T * ga_ref[0, pl.ds(q0, tq), :]).astype(BF16)

    assert n_k % 2 == 0 and n_k >= 2

    def q_tile_body(qi, cms_diag):
        buf = qi % 2

        def pipe(jj, carry):
            m_run, cms = carry
            i = 2 * jj
            cms_odd = scores(seq(i + 1, qi), qi, buf, 1, False)
            m_run = softmax_pv(seq(i, qi), qi, 0, m_run, cms)
            cms_even = scores(seq(i + 2, qi), qi, buf, 0, False)
            m_run = softmax_pv(seq(i + 1, qi), qi, 1, m_run, cms_odd)
            return m_run, cms_even

        m_run, cms = lax.fori_loop(0, (n_k - 2) // 2, pipe, ((neg_big, neg_big), cms_diag), unroll=True)
        cms_last = scores(seq(n_k - 1, qi), qi, buf, 1, False)
        m_run = softmax_pv(seq(n_k - 2, qi), qi, 0, m_run, cms)
        q_next = jnp.minimum(qi + 1, n_q - 1)
        build_weights(q_next, 1 - buf)
        cms_diag_next = scores(q_next, q_next, 1 - buf, 0, True)
        softmax_pv(seq(n_k - 1, qi), qi, 1, m_run, cms_last)
        finalize(qi)
        return cms_diag_next

    acc1_ref[...] = jnp.zeros_like(acc1_ref)
    acc2_ref[...] = jnp.zeros_like(acc2_ref)
    build_weights(0, 0)
    lax.fori_loop(0, n_q, q_tile_body, scores(0, 0, 0, 0, True))


def _attention(qt, kk, vt, ga, sg, lq1, lk1, lq2, lk2, lambda_init):
    b, n_q, _, tq = qt.shape
    _, n_k, _, tk = vt.shape
    s = kk.shape[1]
    assert tq == Q_TILE and tk == K_TILE and n_q * tq == s and n_k * tk == s
    assert s <= 1 << (POS_SPLIT_BITS + 8)
    full = lambda a: pl.BlockSpec(a.shape, lambda bi, hi: (0,) * a.ndim)
    t_spec = lambda nblk, width: pl.BlockSpec((1, nblk, V_DIM, width), lambda bi, hi: (bi, 0, hi, 0))
    r_spec = pl.BlockSpec((1, s, V_DIM), lambda bi, hi: (bi, 0, hi))
    return pl.pallas_call(
        functools.partial(_attn_kernel, lambda_init=lambda_init),
        grid=(b, N_HEADS),
        in_specs=[t_spec(n_q, tq), r_spec, t_spec(n_k, tk), r_spec,
                  full(sg), full(lq1), full(lk1), full(lq2), full(lk2)],
        out_specs=r_spec,
        out_shape=jax.ShapeDtypeStruct((b, s, ATTN_WIDTH), BF16),
        scratch_shapes=[
            pltpu.VMEM((s, 128), BF16),
            pltpu.VMEM((2, 3, 2, 256, Q_TILE), BF16),
            pltpu.VMEM((K_TILE, Q_TILE), F32),
            pltpu.VMEM((2, 2, K_TILE, Q_TILE), F32),
            pltpu.VMEM((V_DIM + ONES_ROWS, Q_TILE), F32),
            pltpu.VMEM((V_DIM + ONES_ROWS, Q_TILE), F32),
        ],
        compiler_params=pltpu.CompilerParams(
            dimension_semantics=("arbitrary", "arbitrary"), vmem_limit_bytes=VMEM_LIMIT_BYTES),
        name="diff_attn",
    )(qt, kk, vt, ga, sg, lq1, lk1, lq2, lk2)


def _outproj_kernel(x_ref, oa_ref, cg_ref, u_ref, up_ref, un_ref, p_ref, cw_ref, wo_ref,
                    g2_ref, wg_ref, wp_ref, gf_ref, out_ref, *, tiles_per_seq, final_norm):
    i = pl.program_id(0)
    tm = x_ref.shape[0]
    u = u_ref[...]
    row = lax.broadcasted_iota(jnp.int32, (tm, 1), 0)
    t_in_seq = i % tiles_per_seq
    prev_row = jnp.where(t_in_seq == 0, 0.0, up_ref[7:8, :])
    next_row = jnp.where(t_in_seq == tiles_per_seq - 1, 0.0, un_ref[0:1, :])
    u_prev = jnp.where(row == 0, prev_row, pltpu.roll(u, 1, axis=0))
    u_next = jnp.where(row == tm - 1, next_row, pltpu.roll(u, tm - 1, axis=0))
    cw = cw_ref[...]
    conv = cw[0:1] * u_prev + cw[1:2] * u + cw[2:3] * u_next
    o_conv = (cg_ref[...] * conv).astype(BF16)
    aw = oa_ref.shape[1]
    mix = (jnp.dot(oa_ref[...], wo_ref[0:aw, :], preferred_element_type=F32)
           + jnp.dot(o_conv, wo_ref[aw:, :], preferred_element_type=F32))
    x1 = x_ref[...] + mix
    ms1 = jnp.mean(x1 * x1, axis=-1, keepdims=True)
    n1 = (x1 * lax.rsqrt(ms1 + EPS) * g2_ref[...]).astype(BF16)
    gate = jax.nn.sigmoid(jnp.dot(n1, wg_ref[...], preferred_element_type=F32))
    pp = jnp.dot(p_ref[...].astype(BF16), wp_ref[...], preferred_element_type=F32)
    x2 = x1 + gate * pp
    if final_norm:
        ms2 = jnp.mean(x2 * x2, axis=-1, keepdims=True)
        x2 = x2 * lax.rsqrt(ms2 + EPS) * gf_ref[...]
    out_ref[...] = x2


def _outproj(xf, oa, cg, u, pf, cw, wo, g2, wg, wp, gf, seq_len, final_norm):
    n, d = xf.shape
    tm = ROW_TILE
    nt = n // tm
    sub = 8
    nsub = n // sub
    row_spec = lambda width: pl.BlockSpec((tm, width), lambda i: (i, 0))
    full = lambda a: pl.BlockSpec(a.shape, lambda i: (0,) * a.ndim)
    prev_spec = pl.BlockSpec((sub, CONV_WIDTH), lambda i: (jnp.maximum(i * (tm // sub) - 1, 0), 0))
    next_spec = pl.BlockSpec((sub, CONV_WIDTH), lambda i: (jnp.minimum((i + 1) * (tm // sub), nsub - 1), 0))
    return pl.pallas_call(
        functools.partial(_outproj_kernel, tiles_per_seq=seq_len // tm, final_norm=final_norm),
        grid=(nt,),
        in_specs=[row_spec(d), row_spec(ATTN_WIDTH), row_spec(CONV_WIDTH), row_spec(CONV_WIDTH),
                  prev_spec, next_spec, row_spec(pf.shape[1]),
                  full(cw), full(wo), full(g2), full(wg), full(wp), full(gf)],
        out_specs=row_spec(d),
        out_shape=jax.ShapeDtypeStruct((n, d), F32),
        compiler_params=pltpu.CompilerParams(
            dimension_semantics=("arbitrary",), vmem_limit_bytes=VMEM_LIMIT_BYTES),
        name="outproj",
    )(xf, oa, cg, u, u, u, pf, cw, wo, g2, wg, wp, gf)


def kernel(x, p, mix_norm_g, w_in, lambda_q1, lambda_k1, lambda_q2, lambda_k2, subln_g, conv_w,
           w_out, ple_norm_g, w_ple_gate, w_ple_proj, final_norm_g):
    b, s, d = x.shape
    depth = w_in.shape[0]
    assert s % ROW_TILE == 0 and ROW_TILE == Q_TILE == K_TILE
    aw = ATTN_WIDTH
    xf = x.reshape(b * s, d)
    gf = final_norm_g.reshape(1, d)
    for i in range(depth):
        lambda_init = 0.8 - 0.6 * math.exp(-0.3 * i)
        wi = w_in[i]
        wn = jnp.concatenate([wi[:, aw:2 * aw], wi[:, 3 * aw:]], axis=1).astype(BF16)
        wqt = (wi[:, 0:aw] * (HEAD_DIM ** -0.5 * LOG2E)).T.astype(BF16)
        wvt = wi[:, 2 * aw:3 * aw].T.astype(BF16)
        kk, ga, cg, u, qt, vt = _inproj(xf, mix_norm_g[i].reshape(1, d), wn, wqt, wvt,
                                        1.0 - lambda_init)
        n_t = s // ROW_TILE
        oa = _attention(qt.reshape(b, n_t, aw, ROW_TILE), kk.reshape(b, s, aw),
                        vt.reshape(b, n_t, aw, ROW_TILE), ga.reshape(b, s, aw),
                        subln_g[i].reshape(V_DIM, 1),
                        lambda_q1[i].reshape(1, -1), lambda_k1[i].reshape(1, -1),
                        lambda_q2[i].reshape(1, -1), lambda_k2[i].reshape(1, -1), lambda_init)
        xf = _outproj(xf, oa.reshape(b * s, aw), cg, u, p[i].reshape(b * s, -1), conv_w[i],
                      w_out[i].astype(BF16), ple_norm_g[i].reshape(1, d),
                      w_ple_gate[i].astype(BF16), w_ple_proj[i].astype(BF16), gf, s,
                      final_norm=(i == depth - 1))
    return xf.reshape(b, s, d)
```

```python
import functools
import math

import jax
import jax.numpy as jnp
from jax import lax
from jax.experimental import pallas as pl
from jax.experimental.pallas import tpu as pltpu

F32 = jnp.float32
BF16 = jnp.bfloat16

N_HEADS = 4
HEAD_DIM = 64
V_DIM = 2 * HEAD_DIM
ATTN_WIDTH = N_HEADS * V_DIM
CONV_WIDTH = 512
EPS = 1e-6

ROW_TILE = 512
Q_TILE = 512
K_TILE = 512
POS_SPLIT_BITS = 6
SLOPE_PIECES = 3
SLOPE_ROWS = 16
LOG2E = 1.4426950408889634
ONES_ROWS = 16
NEG_BIG = -1e30
VMEM_LIMIT_BYTES = 48 * 1024 * 1024


def _silu(v):
    return v * jax.nn.sigmoid(v)


def _inproj_kernel(x_ref, g_ref, wn_ref, wqt_ref, wvt_ref,
                   kk_ref, ga_ref, cg_ref, u_ref, qt_ref, vt_ref, *, gate_scale):
    x = x_ref[...]
    ms = jnp.mean(x * x, axis=-1, keepdims=True)
    h = (x * lax.rsqrt(ms + EPS) * g_ref[...]).astype(BF16)
    zn = jnp.dot(h, wn_ref[...], preferred_element_type=F32)
    w = ATTN_WIDTH
    kk_ref[...] = zn[:, 0:w].astype(BF16)
    ga_ref[...] = _silu(zn[:, w:2 * w]) * gate_scale
    cb = zn[:, 2 * w:3 * w]
    cc = zn[:, 3 * w:4 * w]
    ch = zn[:, 4 * w:5 * w]
    gc = zn[:, 5 * w:6 * w]
    cg_ref[...] = cb * _silu(gc)
    u_ref[...] = cc * ch
    nt = (((1,), (1,)), ((), ()))
    qt_ref[0] = lax.dot_general(wqt_ref[...], h, nt, preferred_element_type=F32).astype(BF16)
    vt_ref[0] = lax.dot_general(wvt_ref[...], h, nt, preferred_element_type=F32).astype(BF16)


def _inproj(xf, g, wn, wqt, wvt, gate_scale):
    n, d = xf.shape
    tm = ROW_TILE
    nt = n // tm
    w = ATTN_WIDTH
    row_spec = lambda width: pl.BlockSpec((tm, width), lambda i: (i, 0))
    full = lambda a: pl.BlockSpec(a.shape, lambda i: (0,) * a.ndim)
    t_spec = pl.BlockSpec((1, w, tm), lambda i: (i, 0, 0))
    return pl.pallas_call(
        functools.partial(_inproj_kernel, gate_scale=gate_scale),
        grid=(nt,),
        in_specs=[row_spec(d), full(g), full(wn), full(wqt), full(wvt)],
        out_specs=[row_spec(w), row_spec(w), row_spec(w), row_spec(w), t_spec, t_spec],
        out_shape=[
            jax.ShapeDtypeStruct((n, w), BF16),
            jax.ShapeDtypeStruct((n, w), F32),
            jax.ShapeDtypeStruct((n, w), F32),
            jax.ShapeDtypeStruct((n, w), F32),
            jax.ShapeDtypeStruct((nt, w, tm), BF16),
            jax.ShapeDtypeStruct((nt, w, tm), BF16),
        ],
        compiler_params=pltpu.CompilerParams(
            dimension_semantics=("arbitrary",), vmem_limit_bytes=VMEM_LIMIT_BYTES),
        name="inproj",
    )(xf, g, wn, wqt, wvt)


LEFT, DIAG, RIGHT = 0, 1, 2
SIGMA = {LEFT: 1.0, DIAG: 0.0, RIGHT: -1.0}


def _attn_kernel(qt_ref, kk_ref, vt_ref, ga_ref, sg_ref, lq1_ref, lk1_ref, lq2_ref, lk2_ref,
                 o_ref, pos_ref, w_ref, dist_ref, bias_ref, s_ref, acc1_ref, acc2_ref, *, lambda_init):
    n_q = qt_ref.shape[1]
    n_k = vt_ref.shape[1]
    tq, tk = Q_TILE, K_TILE
    head = pl.program_id(1)

    slope_bits = (127 - 2 * (head + 1)) << 23
    slope = lax.bitcast_convert_type(jnp.full((1, 1), slope_bits, jnp.int32), F32)

    lam = (jnp.exp(jnp.sum(lq1_ref[...] * lk1_ref[...], axis=-1, keepdims=True))
           - jnp.exp(jnp.sum(lq2_ref[...] * lk2_ref[...], axis=-1, keepdims=True))
           + lambda_init)

    slope2 = slope * LOG2E

    @pl.when((pl.program_id(0) == 0) & (head == 0))
    def _init_head_independent_scratch():
        s_len = pos_ref.shape[0]
        kpos = lax.broadcasted_iota(jnp.int32, (s_len, 128), 0)
        lane = lax.broadcasted_iota(jnp.int32, (s_len, 128), 1)
        hi = ((kpos >> POS_SPLIT_BITS) << POS_SPLIT_BITS).astype(F32)
        lo = (kpos & ((1 << POS_SPLIT_BITS) - 1)).astype(F32)
        pos_ref[...] = jnp.where(lane < SLOPE_PIECES, hi,
                                 jnp.where(lane < 2 * SLOPE_PIECES, lo, 0.0)).astype(BF16)
        di = (lax.broadcasted_iota(jnp.int32, (tk, tq), 0)
              - lax.broadcasted_iota(jnp.int32, (tk, tq), 1))
        dist_ref[...] = jnp.abs(di).astype(F32)
        zeros = jnp.zeros((128 - SLOPE_ROWS, tq), BF16)
        for buf in range(2):
            for v in (LEFT, DIAG, RIGHT):
                for t in range(2):
                    w_ref[buf, v, t, 128 + SLOPE_ROWS:256, :] = zeros

    bias_ref[...] = -slope2 * dist_ref[...]

    ones_rows = jnp.where(lax.broadcasted_iota(jnp.int32, (ONES_ROWS, tk), 0) == 0, 1.0, 0.0).astype(BF16)
    row128 = lax.broadcasted_iota(jnp.int32, (128, tq), 0)
    qlane = lax.broadcasted_iota(jnp.int32, (1, tq), 1)

    neg_big = jnp.full((1, tq), NEG_BIG, F32)
    accs = (acc1_ref, acc2_ref)

    slope_row = lax.broadcasted_iota(jnp.int32, (SLOPE_ROWS, tq), 0)
    rest = jnp.broadcast_to(slope2, (SLOPE_ROWS, tq))
    pieces = []
    for _ in range(SLOPE_PIECES):
        piece = rest.astype(BF16).astype(F32)
        pieces.append(piece)
        rest = rest - piece
    piece_rows = pieces[SLOPE_PIECES - 1]
    for i in range(SLOPE_PIECES - 1):
        piece_rows = jnp.where((slope_row == i) | (slope_row == i + SLOPE_PIECES), pieces[i], piece_rows)
    for v in (LEFT, DIAG, RIGHT):
        rows = jnp.where(slope_row < 2 * SLOPE_PIECES, SIGMA[v] * piece_rows, 0.0).astype(BF16)
        for buf in range(2):
            for t in range(2):
                w_ref[buf, v, t, 128:128 + SLOPE_ROWS, :] = rows

    def build_weights(qi, buf):
        qq = qt_ref[0, qi].astype(F32)
        tops = (jnp.where(row128 < HEAD_DIM, qq, 0.0).astype(BF16),
                jnp.where(row128 >= HEAD_DIM, qq, 0.0).astype(BF16))
        for v in (LEFT, DIAG, RIGHT):
            for t in range(2):
                w_ref[buf, v, t, 0:128, :] = tops[t]

    def seq(i, qi):
        j = i - 1
        return jnp.where(i == 0, qi, j + jnp.where(j >= qi, 1, 0))

    def shift_of(c, qi):
        side = jnp.where(c < qi, -1.0, jnp.where(c > qi, 1.0, 0.0))
        return side * ((qlane + qi * tq).astype(F32) * slope2)

    def scores(c, qi, buf, slot, diag):
        k0 = c * tk if isinstance(c, int) else pl.multiple_of(c * tk, tk)
        lhs = jnp.concatenate([kk_ref[0, pl.ds(k0, tk), :], pos_ref[pl.ds(k0, tk), :]], axis=1)
        v = DIAG if diag else jnp.where(c < qi, LEFT, RIGHT)
        cms = []
        for t in range(2):
            s = jnp.dot(lhs, w_ref[buf, v, t], preferred_element_type=F32)
            if diag:
                s = s + bias_ref[...]
            s_ref[slot, t] = s
            cm = jnp.max(s, axis=0, keepdims=True)
            cms.append(cm if diag else cm + shift_of(c, qi))
        return tuple(cms)

    def softmax_pv(c, qi, slot, m_run, cms):
        vt = jnp.concatenate([vt_ref[0, c], ones_rows], axis=0)
        shift = shift_of(c, qi)
        new = []
        for t in range(2):
            m_new = jnp.maximum(m_run[t], cms[t])
            alpha = jnp.exp2(m_run[t] - m_new)
            p = jnp.exp2(s_ref[slot, t] - (m_new - shift)).astype(BF16)
            pv = jnp.dot(vt, p, preferred_element_type=F32)
            accs[t][...] = alpha * accs[t][...] + pv
            new.append(m_new)
        return tuple(new)

    def finalize(qi):
        q0 = pl.multiple_of(qi * tq, tq)
        a1 = acc1_ref[...]
        a2 = acc2_ref[...]
        att = a1[0:V_DIM] / a1[V_DIM:V_DIM + 1] - lam * (a2[0:V_DIM] / a2[V_DIM:V_DIM + 1])
        ms = jnp.mean(att * att, axis=0, keepdims=True)
        y = att * lax.rsqrt(ms + EPS) * sg_ref[...]
        o_ref[0, pl.ds(q0, tq), :] = (y.T * ga_ref[0, pl.ds(q0, tq), :]).astype(BF16)

    assert n_k % 2 == 0 and n_k >= 2

    def q_tile_body(qi, cms_diag):
        buf = qi % 2

        def pipe(jj, carry):
            m_run, cms = carry
            i = 2 * jj
            cms_odd = scores(seq(i + 1, qi), qi, buf, 1, False)
            m_run = softmax_pv(seq(i, qi), qi, 0, m_run, cms)
            cms_even = scores(seq(i + 2, qi), qi, buf, 0, False)
            m_run = softmax_pv(seq(i + 1, qi), qi, 1, m_run, cms_odd)
            return m_run, cms_even

        m_run, cms = lax.fori_loop(0, (n_k - 2) // 2, pipe, ((neg_big, neg_big), cms_diag), unroll=True)
        cms_last = scores(seq(n_k - 1, qi), qi, buf, 1, False)
        m_run = softmax_pv(seq(n_k - 2, qi), qi, 0, m_run, cms)
        q_next = jnp.minimum(qi + 1, n_q - 1)
        build_weights(q_next, 1 - buf)
        cms_diag_next = scores(q_next, q_next, 1 - buf, 0, True)
        softmax_pv(seq(n_k - 1, qi), qi, 1, m_run, cms_last)
        finalize(qi)
        return cms_diag_next

    acc1_ref[...] = jnp.zeros_like(acc1_ref)
    acc2_ref[...] = jnp.zeros_like(acc2_ref)
    build_weights(0, 0)
    lax.fori_loop(0, n_q, q_tile_body, scores(0, 0, 0, 0, True))


def _attention(qt, kk, vt, ga, sg, lq1, lk1, lq2, lk2, lambda_init):
    b, n_q, _, tq = qt.shape
    _, n_k, _, tk = vt.shape
    s = kk.shape[1]
    assert tq == Q_TILE and tk == K_TILE and n_q * tq == s and n_k * tk == s
    assert s <= 1 << (POS_SPLIT_BITS + 8)
    full = lambda a: pl.BlockSpec(a.shape, lambda bi, hi: (0,) * a.ndim)
    t_spec = lambda nblk, width: pl.BlockSpec((1, nblk, V_DIM, width), lambda bi, hi: (bi, 0, hi, 0))
    r_spec = pl.BlockSpec((1, s, V_DIM), lambda bi, hi: (bi, 0, hi))
    return pl.pallas_call(
        functools.partial(_attn_kernel, lambda_init=lambda_init),
        grid=(b, N_HEADS),
        in_specs=[t_spec(n_q, tq), r_spec, t_spec(n_k, tk), r_spec,
                  full(sg), full(lq1), full(lk1), full(lq2), full(lk2)],
        out_specs=r_spec,
        out_shape=jax.ShapeDtypeStruct((b, s, ATTN_WIDTH), BF16),
        scratch_shapes=[
            pltpu.VMEM((s, 128), BF16),
            pltpu.VMEM((2, 3, 2, 256, Q_TILE), BF16),
            pltpu.VMEM((K_TILE, Q_TILE), F32),
            pltpu.VMEM((K_TILE, Q_TILE), F32),
            pltpu.VMEM((2, 2, K_TILE, Q_TILE), F32),
            pltpu.VMEM((V_DIM + ONES_ROWS, Q_TILE), F32),
            pltpu.VMEM((V_DIM + ONES_ROWS, Q_TILE), F32),
        ],
        compiler_params=pltpu.CompilerParams(
            dimension_semantics=("arbitrary", "arbitrary"), vmem_limit_bytes=VMEM_LIMIT_BYTES),
        name="diff_attn",
    )(qt, kk, vt, ga, sg, lq1, lk1, lq2, lk2)


def _outproj_kernel(x_ref, oa_ref, cg_ref, u_ref, up_ref, un_ref, p_ref, cw_ref, wo32_ref,
                    g2_ref, wg32_ref, wp32_ref, gf_ref, out_ref, wo_ref, wg_ref, wp_ref,
                    *, tiles_per_seq, final_norm):
    i = pl.program_id(0)
    tm = x_ref.shape[0]

    @pl.when(i == 0)
    def _cast_weights_once():
        wo_ref[...] = wo32_ref[...].astype(BF16)
        wg_ref[...] = wg32_ref[...].astype(BF16)
        wp_ref[...] = wp32_ref[...].astype(BF16)

    u = u_ref[...]
    row = lax.broadcasted_iota(jnp.int32, (tm, 1), 0)
    t_in_seq = i % tiles_per_seq
    prev_row = jnp.where(t_in_seq == 0, 0.0, up_ref[7:8, :])
    next_row = jnp.where(t_in_seq == tiles_per_seq - 1, 0.0, un_ref[0:1, :])
    u_prev = jnp.where(row == 0, prev_row, pltpu.roll(u, 1, axis=0))
    u_next = jnp.where(row == tm - 1, next_row, pltpu.roll(u, tm - 1, axis=0))
    cw = cw_ref[...]
    conv = cw[0:1] * u_prev + cw[1:2] * u + cw[2:3] * u_next
    o_conv = (cg_ref[...] * conv).astype(BF16)
    aw = oa_ref.shape[1]
    mix = (jnp.dot(oa_ref[...], wo_ref[0:aw, :], preferred_element_type=F32)
           + jnp.dot(o_conv, wo_ref[aw:, :], preferred_element_type=F32))
    x1 = x_ref[...] + mix
    ms1 = jnp.mean(x1 * x1, axis=-1, keepdims=True)
    n1 = (x1 * lax.rsqrt(ms1 + EPS) * g2_ref[...]).astype(BF16)
    gate = jax.nn.sigmoid(jnp.dot(n1, wg_ref[...], preferred_element_type=F32))
    pp = jnp.dot(p_ref[...].astype(BF16), wp_ref[...], preferred_element_type=F32)
    x2 = x1 + gate * pp
    if final_norm:
        ms2 = jnp.mean(x2 * x2, axis=-1, keepdims=True)
        x2 = x2 * lax.rsqrt(ms2 + EPS) * gf_ref[...]
    out_ref[...] = x2


def _outproj(xf, oa, cg, u, pf, cw, wo, g2, wg, wp, gf, seq_len, final_norm):
    n, d = xf.shape
    tm = ROW_TILE
    nt = n // tm
    sub = 8
    nsub = n // sub
    row_spec = lambda width: pl.BlockSpec((tm, width), lambda i: (i, 0))
    full = lambda a: pl.BlockSpec(a.shape, lambda i: (0,) * a.ndim)
    prev_spec = pl.BlockSpec((sub, CONV_WIDTH), lambda i: (jnp.maximum(i * (tm // sub) - 1, 0), 0))
    next_spec = pl.BlockSpec((sub, CONV_WIDTH), lambda i: (jnp.minimum((i + 1) * (tm // sub), nsub - 1), 0))
    once = lambda a: pl.BlockSpec(a.shape, lambda i: (0,) * a.ndim, pipeline_mode=pl.Buffered(1))
    return pl.pallas_call(
        functools.partial(_outproj_kernel, tiles_per_seq=seq_len // tm, final_norm=final_norm),
        grid=(nt,),
        in_specs=[row_spec(d), row_spec(ATTN_WIDTH), row_spec(CONV_WIDTH), row_spec(CONV_WIDTH),
                  prev_spec, next_spec, row_spec(pf.shape[1]),
                  full(cw), once(wo), full(g2), once(wg), once(wp), full(gf)],
        out_specs=row_spec(d),
        out_shape=jax.ShapeDtypeStruct((n, d), F32),
        scratch_shapes=[pltpu.VMEM(wo.shape, BF16), pltpu.VMEM(wg.shape, BF16),
                        pltpu.VMEM(wp.shape, BF16)],
        compiler_params=pltpu.CompilerParams(
            dimension_semantics=("arbitrary",), vmem_limit_bytes=VMEM_LIMIT_BYTES),
        name="outproj",
    )(xf, oa, cg, u, u, u, pf, cw, wo, g2, wg, wp, gf)


Q_GROUP, V_GROUP = 0, 2


def _wprep_kernel(w_ref, wn_ref, wqt_ref, wvt_ref, *, q_scale):
    j = pl.program_id(0)

    @pl.when(j == Q_GROUP)
    def _():
        wqt_ref[...] = (w_ref[...] * q_scale).T.astype(BF16)

    @pl.when(j == V_GROUP)
    def _():
        wvt_ref[...] = w_ref[...].T.astype(BF16)

    @pl.when((j != Q_GROUP) & (j != V_GROUP))
    def _():
        wn_ref[...] = w_ref[...].astype(BF16)


def _wprep(wi, q_scale):
    d, n_cols = wi.shape
    w = ATTN_WIDTH
    n_groups = n_cols // w
    nat_block = lambda j: jnp.where(j <= Q_GROUP + 1, 0, jnp.where(j <= V_GROUP + 1, 1, j - 2))
    return pl.pallas_call(
        functools.partial(_wprep_kernel, q_scale=q_scale),
        grid=(n_groups,),
        in_specs=[pl.BlockSpec((d, w), lambda j: (0, j))],
        out_specs=[pl.BlockSpec((d, w), lambda j: (0, nat_block(j))),
                   pl.BlockSpec((w, d), lambda j: (0, 0)),
                   pl.BlockSpec((w, d), lambda j: (0, 0))],
        out_shape=[jax.ShapeDtypeStruct((d, n_cols - 2 * w), BF16),
                   jax.ShapeDtypeStruct((w, d), BF16),
                   jax.ShapeDtypeStruct((w, d), BF16)],
        compiler_params=pltpu.CompilerParams(
            dimension_semantics=("arbitrary",), vmem_limit_bytes=VMEM_LIMIT_BYTES),
        name="wprep",
    )(wi)


def kernel(x, p, mix_norm_g, w_in, lambda_q1, lambda_k1, lambda_q2, lambda_k2, subln_g, conv_w,
           w_out, ple_norm_g, w_ple_gate, w_ple_proj, final_norm_g):
    b, s, d = x.shape
    depth = w_in.shape[0]
    assert s % ROW_TILE == 0 and ROW_TILE == Q_TILE == K_TILE
    aw = ATTN_WIDTH
    xf = x.reshape(b * s, d)
    gf = final_norm_g.reshape(1, d)
    for i in range(depth):
        lambda_init = 0.8 - 0.6 * math.exp(-0.3 * i)
        wn, wqt, wvt = _wprep(w_in[i], HEAD_DIM ** -0.5 * LOG2E)
        kk, ga, cg, u, qt, vt = _inproj(xf, mix_norm_g[i].reshape(1, d), wn, wqt, wvt,
                                        1.0 - lambda_init)
        n_t = s // ROW_TILE
        oa = _attention(qt.reshape(b, n_t, aw, ROW_TILE), kk.reshape(b, s, aw),
                        vt.reshape(b, n_t, aw, ROW_TILE), ga.reshape(b, s, aw),
                        subln_g[i].reshape(V_DIM, 1),
                        lambda_q1[i].reshape(1, -1), lambda_k1[i].reshape(1, -1),
                        lambda_q2[i].reshape(1, -1), lambda_k2[i].reshape(1, -1), lambda_init)
        xf = _outproj(xf, oa.reshape(b * s, aw), cg, u, p[i].reshape(b * s, -1), conv_w[i],
                      w_out[i], ple_norm_g[i].reshape(1, d), w_ple_gate[i], w_ple_proj[i], gf, s,
                      final_norm=(i == depth - 1))
    return xf.reshape(b, s, d)
```

```python
import functools
import math

import jax
import jax.numpy as jnp
from jax import lax
from jax.experimental import pallas as pl
from jax.experimental.pallas import tpu as pltpu

F32 = jnp.float32
BF16 = jnp.bfloat16

N_HEADS = 4
HEAD_DIM = 64
V_DIM = 2 * HEAD_DIM
ATTN_WIDTH = N_HEADS * V_DIM
CONV_WIDTH = 512
EPS = 1e-6

ROW_TILE = 512
Q_TILE = 512
K_TILE = 512
POS_SPLIT_BITS = 6
SLOPE_PIECES = 3
SLOPE_ROWS = 16
LOG2E = 1.4426950408889634
ONES_ROWS = 16
NEG_BIG = -1e30
VMEM_LIMIT_BYTES = 48 * 1024 * 1024


def _silu(v):
    return v * jax.nn.sigmoid(v)


def _inproj_kernel(x_ref, g_ref, wn_ref, wqt_ref, wvt_ref,
                   kk_ref, ga_ref, cg_ref, u_ref, qt_ref, vt_ref, *, gate_scale):
    x = x_ref[...]
    ms = jnp.mean(x * x, axis=-1, keepdims=True)
    h = (x * lax.rsqrt(ms + EPS) * g_ref[...]).astype(BF16)
    zn = jnp.dot(h, wn_ref[...], preferred_element_type=F32)
    w = ATTN_WIDTH
    kk_ref[...] = zn[:, 0:w].astype(BF16)
    ga_ref[...] = _silu(zn[:, w:2 * w]) * gate_scale
    cb = zn[:, 2 * w:3 * w]
    cc = zn[:, 3 * w:4 * w]
    ch = zn[:, 4 * w:5 * w]
    gc = zn[:, 5 * w:6 * w]
    cg_ref[...] = cb * _silu(gc)
    u_ref[...] = cc * ch
    nt = (((1,), (1,)), ((), ()))
    qt_ref[0] = lax.dot_general(wqt_ref[...], h, nt, preferred_element_type=F32).astype(BF16)
    vt_ref[0] = lax.dot_general(wvt_ref[...], h, nt, preferred_element_type=F32).astype(BF16)


def _inproj(xf, g, wn, wqt, wvt, gate_scale):
    n, d = xf.shape
    tm = ROW_TILE
    nt = n // tm
    w = ATTN_WIDTH
    row_spec = lambda width: pl.BlockSpec((tm, width), lambda i: (i, 0))
    full = lambda a: pl.BlockSpec(a.shape, lambda i: (0,) * a.ndim)
    t_spec = pl.BlockSpec((1, w, tm), lambda i: (i, 0, 0))
    return pl.pallas_call(
        functools.partial(_inproj_kernel, gate_scale=gate_scale),
        grid=(nt,),
        in_specs=[row_spec(d), full(g), full(wn), full(wqt), full(wvt)],
        out_specs=[row_spec(w), row_spec(w), row_spec(w), row_spec(w), t_spec, t_spec],
        out_shape=[
            jax.ShapeDtypeStruct((n, w), BF16),
            jax.ShapeDtypeStruct((n, w), F32),
            jax.ShapeDtypeStruct((n, w), F32),
            jax.ShapeDtypeStruct((n, w), F32),
            jax.ShapeDtypeStruct((nt, w, tm), BF16),
            jax.ShapeDtypeStruct((nt, w, tm), BF16),
        ],
        compiler_params=pltpu.CompilerParams(
            dimension_semantics=("arbitrary",), vmem_limit_bytes=VMEM_LIMIT_BYTES),
        name="inproj",
    )(xf, g, wn, wqt, wvt)


LEFT, DIAG, RIGHT = 0, 1, 2
SIGMA = {LEFT: 1.0, DIAG: 0.0, RIGHT: -1.0}


def _attn_kernel(qt_ref, kk_ref, vt_ref, ga_ref, sg_ref, lq1_ref, lk1_ref, lq2_ref, lk2_ref,
                 o_ref, pos_ref, w_ref, dist_ref, bias_ref, s_ref, acc1_ref, acc2_ref, *, lambda_init):
    n_q = qt_ref.shape[1]
    n_k = vt_ref.shape[1]
    tq, tk = Q_TILE, K_TILE
    head = pl.program_id(1)

    slope_bits = (127 - 2 * (head + 1)) << 23
    slope = lax.bitcast_convert_type(jnp.full((1, 1), slope_bits, jnp.int32), F32)

    lam = (jnp.exp(jnp.sum(lq1_ref[...] * lk1_ref[...], axis=-1, keepdims=True))
           - jnp.exp(jnp.sum(lq2_ref[...] * lk2_ref[...], axis=-1, keepdims=True))
           + lambda_init)

    slope2 = slope * LOG2E

    @pl.when((pl.program_id(0) == 0) & (head == 0))
    def _init_head_independent_scratch():
        s_len = pos_ref.shape[0]
        kpos = lax.broadcasted_iota(jnp.int32, (s_len, 128), 0)
        lane = lax.broadcasted_iota(jnp.int32, (s_len, 128), 1)
        hi = ((kpos >> POS_SPLIT_BITS) << POS_SPLIT_BITS).astype(F32)
        lo = (kpos & ((1 << POS_SPLIT_BITS) - 1)).astype(F32)
        pos_ref[...] = jnp.where(lane < SLOPE_PIECES, hi,
                                 jnp.where(lane < 2 * SLOPE_PIECES, lo, 0.0)).astype(BF16)
        di = (lax.broadcasted_iota(jnp.int32, (tk, tq), 0)
              - lax.broadcasted_iota(jnp.int32, (tk, tq), 1))
        dist_ref[...] = jnp.abs(di).astype(F32)
        zeros = jnp.zeros((128 - SLOPE_ROWS, tq), BF16)
        for buf in range(2):
            for v in (LEFT, DIAG, RIGHT):
                for t in range(2):
                    w_ref[buf, v, t, 128 + SLOPE_ROWS:256, :] = zeros

    bias_ref[...] = -slope2 * dist_ref[...]

    ones_rows = jnp.where(lax.broadcasted_iota(jnp.int32, (ONES_ROWS, tk), 0) == 0, 1.0, 0.0).astype(BF16)
    row128 = lax.broadcasted_iota(jnp.int32, (128, tq), 0)
    qlane = lax.broadcasted_iota(jnp.int32, (1, tq), 1)

    neg_big = jnp.full((1, tq), NEG_BIG, F32)
    accs = (acc1_ref, acc2_ref)

    slope_row = lax.broadcasted_iota(jnp.int32, (SLOPE_ROWS, tq), 0)
    rest = jnp.broadcast_to(slope2, (SLOPE_ROWS, tq))
    pieces = []
    for _ in range(SLOPE_PIECES):
        piece = rest.astype(BF16).astype(F32)
        pieces.append(piece)
        rest = rest - piece
    piece_rows = pieces[SLOPE_PIECES - 1]
    for i in range(SLOPE_PIECES - 1):
        piece_rows = jnp.where((slope_row == i) | (slope_row == i + SLOPE_PIECES), pieces[i], piece_rows)
    for v in (LEFT, DIAG, RIGHT):
        rows = jnp.where(slope_row < 2 * SLOPE_PIECES, SIGMA[v] * piece_rows, 0.0).astype(BF16)
        for buf in range(2):
            for t in range(2):
                w_ref[buf, v, t, 128:128 + SLOPE_ROWS, :] = rows

    def build_weights(qi, buf):
        qq = qt_ref[0, qi].astype(F32)
        tops = (jnp.where(row128 < HEAD_DIM, qq, 0.0).astype(BF16),
                jnp.where(row128 >= HEAD_DIM, qq, 0.0).astype(BF16))
        for v in (LEFT, DIAG, RIGHT):
            for t in range(2):
                w_ref[buf, v, t, 0:128, :] = tops[t]

    def seq(i, qi):
        j = i - 1
        return jnp.where(i == 0, qi, j + jnp.where(j >= qi, 1, 0))

    def shift_of(c, qi):
        side = jnp.where(c < qi, -1.0, jnp.where(c > qi, 1.0, 0.0))
        return side * ((qlane + qi * tq).astype(F32) * slope2)

    def scores(c, qi, buf, slot, diag):
        k0 = c * tk if isinstance(c, int) else pl.multiple_of(c * tk, tk)
        lhs = jnp.concatenate([kk_ref[0, pl.ds(k0, tk), :], pos_ref[pl.ds(k0, tk), :]], axis=1)
        v = DIAG if diag else jnp.where(c < qi, LEFT, RIGHT)
        cms = []
        for t in range(2):
            s = jnp.dot(lhs, w_ref[buf, v, t], preferred_element_type=F32)
            if diag:
                s = s + bias_ref[...]
            s_ref[slot, t] = s
            cm = jnp.max(s, axis=0, keepdims=True)
            cms.append(cm if diag else cm + shift_of(c, qi))
        return tuple(cms)

    def softmax_pv(c, qi, slot, m_run, cms):
        vt = jnp.concatenate([vt_ref[0, c], ones_rows], axis=0)
        shift = shift_of(c, qi)
        new = []
        for t in range(2):
            m_new = jnp.maximum(m_run[t], cms[t])
            alpha = jnp.exp2(m_run[t] - m_new)
            p = jnp.exp2(s_ref[slot, t] - (m_new - shift)).astype(BF16)
            pv = jnp.dot(vt, p, preferred_element_type=F32)
            accs[t][...] = alpha * accs[t][...] + pv
            new.append(m_new)
        return tuple(new)

    def finalize(qi):
        q0 = pl.multiple_of(qi * tq, tq)
        a1 = acc1_ref[...]
        a2 = acc2_ref[...]
        att = a1[0:V_DIM] / a1[V_DIM:V_DIM + 1] - lam * (a2[0:V_DIM] / a2[V_DIM:V_DIM + 1])
        ms = jnp.mean(att * att, axis=0, keepdims=True)
        y = att * lax.rsqrt(ms + EPS) * sg_ref[...]
        o_ref[0, pl.ds(q0, tq), :] = (y.T * ga_ref[0, pl.ds(q0, tq), :]).astype(BF16)

    assert n_k % 2 == 0 and n_k >= 2

    def q_tile_body(qi, cms_diag):
        buf = qi % 2

        def pipe(jj, carry):
            m_run, cms = carry
            i = 2 * jj
            cms_odd = scores(seq(i + 1, qi), qi, buf, 1, False)
            m_run = softmax_pv(seq(i, qi), qi, 0, m_run, cms)
            cms_even = scores(seq(i + 2, qi), qi, buf, 0, False)
            m_run = softmax_pv(seq(i + 1, qi), qi, 1, m_run, cms_odd)
            return m_run, cms_even

        m_run, cms = lax.fori_loop(0, (n_k - 2) // 2, pipe, ((neg_big, neg_big), cms_diag), unroll=True)
        cms_last = scores(seq(n_k - 1, qi), qi, buf, 1, False)
        m_run = softmax_pv(seq(n_k - 2, qi), qi, 0, m_run, cms)
        q_next = jnp.minimum(qi + 1, n_q - 1)
        build_weights(q_next, 1 - buf)
        cms_diag_next = scores(q_next, q_next, 1 - buf, 0, True)
        softmax_pv(seq(n_k - 1, qi), qi, 1, m_run, cms_last)
        finalize(qi)
        return cms_diag_next

    acc1_ref[...] = jnp.zeros_like(acc1_ref)
    acc2_ref[...] = jnp.zeros_like(acc2_ref)
    build_weights(0, 0)
    lax.fori_loop(0, n_q, q_tile_body, scores(0, 0, 0, 0, True), unroll=2)


def _attention(qt, kk, vt, ga, sg, lq1, lk1, lq2, lk2, lambda_init):
    b, n_q, _, tq = qt.shape
    _, n_k, _, tk = vt.shape
    s = kk.shape[1]
    assert tq == Q_TILE and tk == K_TILE and n_q * tq == s and n_k * tk == s
    assert s <= 1 << (POS_SPLIT_BITS + 8)
    full = lambda a: pl.BlockSpec(a.shape, lambda bi, hi: (0,) * a.ndim)
    t_spec = lambda nblk, width: pl.BlockSpec((1, nblk, V_DIM, width), lambda bi, hi: (bi, 0, hi, 0))
    r_spec = pl.BlockSpec((1, s, V_DIM), lambda bi, hi: (bi, 0, hi))
    return pl.pallas_call(
        functools.partial(_attn_kernel, lambda_init=lambda_init),
        grid=(b, N_HEADS),
        in_specs=[t_spec(n_q, tq), r_spec, t_spec(n_k, tk), r_spec,
                  full(sg), full(lq1), full(lk1), full(lq2), full(lk2)],
        out_specs=r_spec,
        out_shape=jax.ShapeDtypeStruct((b, s, ATTN_WIDTH), BF16),
        scratch_shapes=[
            pltpu.VMEM((s, 128), BF16),
            pltpu.VMEM((2, 3, 2, 256, Q_TILE), BF16),
            pltpu.VMEM((K_TILE, Q_TILE), F32),
            pltpu.VMEM((K_TILE, Q_TILE), F32),
            pltpu.VMEM((2, 2, K_TILE, Q_TILE), F32),
            pltpu.VMEM((V_DIM + ONES_ROWS, Q_TILE), F32),
            pltpu.VMEM((V_DIM + ONES_ROWS, Q_TILE), F32),
        ],
        compiler_params=pltpu.CompilerParams(
            dimension_semantics=("arbitrary", "arbitrary"), vmem_limit_bytes=VMEM_LIMIT_BYTES),
        name="diff_attn",
    )(qt, kk, vt, ga, sg, lq1, lk1, lq2, lk2)


def _outproj_kernel(x_ref, oa_ref, cg_ref, u_ref, up_ref, un_ref, p_ref, cw_ref, wo32_ref,
                    g2_ref, wg32_ref, wp32_ref, gf_ref, out_ref, wo_ref, wg_ref, wp_ref,
                    *, tiles_per_seq, final_norm):
    i = pl.program_id(0)
    tm = x_ref.shape[0]

    @pl.when(i == 0)
    def _cast_weights_once():
        wo_ref[...] = wo32_ref[...].astype(BF16)
        wg_ref[...] = wg32_ref[...].astype(BF16)
        wp_ref[...] = wp32_ref[...].astype(BF16)

    u = u_ref[...]
    row = lax.broadcasted_iota(jnp.int32, (tm, 1), 0)
    t_in_seq = i % tiles_per_seq
    prev_row = jnp.where(t_in_seq == 0, 0.0, up_ref[7:8, :])
    next_row = jnp.where(t_in_seq == tiles_per_seq - 1, 0.0, un_ref[0:1, :])
    u_prev = jnp.where(row == 0, prev_row, pltpu.roll(u, 1, axis=0))
    u_next = jnp.where(row == tm - 1, next_row, pltpu.roll(u, tm - 1, axis=0))
    cw = cw_ref[...]
    conv = cw[0:1] * u_prev + cw[1:2] * u + cw[2:3] * u_next
    o_conv = (cg_ref[...] * conv).astype(BF16)
    aw = oa_ref.shape[1]
    mix = (jnp.dot(oa_ref[...], wo_ref[0:aw, :], preferred_element_type=F32)
           + jnp.dot(o_conv, wo_ref[aw:, :], preferred_element_type=F32))
    x1 = x_ref[...] + mix
    ms1 = jnp.mean(x1 * x1, axis=-1, keepdims=True)
    n1 = (x1 * lax.rsqrt(ms1 + EPS) * g2_ref[...]).astype(BF16)
    gate = jax.nn.sigmoid(jnp.dot(n1, wg_ref[...], preferred_element_type=F32))
    pp = jnp.dot(p_ref[...].astype(BF16), wp_ref[...], preferred_element_type=F32)
    x2 = x1 + gate * pp
    if final_norm:
        ms2 = jnp.mean(x2 * x2, axis=-1, keepdims=True)
        x2 = x2 * lax.rsqrt(ms2 + EPS) * gf_ref[...]
    out_ref[...] = x2


def _outproj(xf, oa, cg, u, pf, cw, wo, g2, wg, wp, gf, seq_len, final_norm):
    n, d = xf.shape
    tm = ROW_TILE
    nt = n // tm
    sub = 8
    nsub = n // sub
    row_spec = lambda width: pl.BlockSpec((tm, width), lambda i: (i, 0))
    full = lambda a: pl.BlockSpec(a.shape, lambda i: (0,) * a.ndim)
    prev_spec = pl.BlockSpec((sub, CONV_WIDTH), lambda i: (jnp.maximum(i * (tm // sub) - 1, 0), 0))
    next_spec = pl.BlockSpec((sub, CONV_WIDTH), lambda i: (jnp.minimum((i + 1) * (tm // sub), nsub - 1), 0))
    once = lambda a: pl.BlockSpec(a.shape, lambda i: (0,) * a.ndim, pipeline_mode=pl.Buffered(1))
    return pl.pallas_call(
        functools.partial(_outproj_kernel, tiles_per_seq=seq_len // tm, final_norm=final_norm),
        grid=(nt,),
        in_specs=[row_spec(d), row_spec(ATTN_WIDTH), row_spec(CONV_WIDTH), row_spec(CONV_WIDTH),
                  prev_spec, next_spec, row_spec(pf.shape[1]),
                  full(cw), once(wo), full(g2), once(wg), once(wp), full(gf)],
        out_specs=row_spec(d),
        out_shape=jax.ShapeDtypeStruct((n, d), F32),
        scratch_shapes=[pltpu.VMEM(wo.shape, BF16), pltpu.VMEM(wg.shape, BF16),
                        pltpu.VMEM(wp.shape, BF16)],
        compiler_params=pltpu.CompilerParams(
            dimension_semantics=("arbitrary",), vmem_limit_bytes=VMEM_LIMIT_BYTES),
        name="outproj",
    )(xf, oa, cg, u, u, u, pf, cw, wo, g2, wg, wp, gf)


Q_GROUP, V_GROUP = 0, 2


def _wprep_kernel(w_ref, wn_ref, wqt_ref, wvt_ref, *, q_scale):
    j = pl.program_id(0)

    @pl.when(j == Q_GROUP)
    def _():
        wqt_ref[...] = (w_ref[...] * q_scale).T.astype(BF16)

    @pl.when(j == V_GROUP)
    def _():
        wvt_ref[...] = w_ref[...].T.astype(BF16)

    @pl.when((j != Q_GROUP) & (j != V_GROUP))
    def _():
        wn_ref[...] = w_ref[...].astype(BF16)


def _wprep(wi, q_scale):
    d, n_cols = wi.shape
    w = ATTN_WIDTH
    n_groups = n_cols // w
    nat_block = lambda j: jnp.where(j <= Q_GROUP + 1, 0, jnp.where(j <= V_GROUP + 1, 1, j - 2))
    return pl.pallas_call(
        functools.partial(_wprep_kernel, q_scale=q_scale),
        grid=(n_groups,),
        in_specs=[pl.BlockSpec((d, w), lambda j: (0, j))],
        out_specs=[pl.BlockSpec((d, w), lambda j: (0, nat_block(j))),
                   pl.BlockSpec((w, d), lambda j: (0, 0)),
                   pl.BlockSpec((w, d), lambda j: (0, 0))],
        out_shape=[jax.ShapeDtypeStruct((d, n_cols - 2 * w), BF16),
                   jax.ShapeDtypeStruct((w, d), BF16),
                   jax.ShapeDtypeStruct((w, d), BF16)],
        compiler_params=pltpu.CompilerParams(
            dimension_semantics=("arbitrary",), vmem_limit_bytes=VMEM_LIMIT_BYTES),
        name="wprep",
    )(wi)


def kernel(x, p, mix_norm_g, w_in, lambda_q1, lambda_k1, lambda_q2, lambda_k2, subln_g, conv_w,
           w_out, ple_norm_g, w_ple_gate, w_ple_proj, final_norm_g):
    b, s, d = x.shape
    depth = w_in.shape[0]
    assert s % ROW_TILE == 0 and ROW_TILE == Q_TILE == K_TILE
    aw = ATTN_WIDTH
    xf = x.reshape(b * s, d)
    gf = final_norm_g.reshape(1, d)
    for i in range(depth):
        lambda_init = 0.8 - 0.6 * math.exp(-0.3 * i)
        wn, wqt, wvt = _wprep(w_in[i], HEAD_DIM ** -0.5 * LOG2E)
        kk, ga, cg, u, qt, vt = _inproj(xf, mix_norm_g[i].reshape(1, d), wn, wqt, wvt,
                                        1.0 - lambda_init)
        n_t = s // ROW_TILE
        oa = _attention(qt.reshape(b, n_t, aw, ROW_TILE), kk.reshape(b, s, aw),
                        vt.reshape(b, n_t, aw, ROW_TILE), ga.reshape(b, s, aw),
                        subln_g[i].reshape(V_DIM, 1),
                        lambda_q1[i].reshape(1, -1), lambda_k1[i].reshape(1, -1),
                        lambda_q2[i].reshape(1, -1), lambda_k2[i].reshape(1, -1), lambda_init)
        xf = _outproj(xf, oa.reshape(b * s, aw), cg, u, p[i].reshape(b * s, -1), conv_w[i],
                      w_out[i], ple_norm_g[i].reshape(1, d), w_ple_gate[i], w_ple_proj[i], gf, s,
                      final_norm=(i == depth - 1))
    return xf.reshape(b, s, d)
```

```python
import functools
import math

import jax
import jax.numpy as jnp
from jax import lax
from jax.experimental import pallas as pl
from jax.experimental.pallas import tpu as pltpu

F32 = jnp.float32
BF16 = jnp.bfloat16

N_HEADS = 4
HEAD_DIM = 64
V_DIM = 2 * HEAD_DIM
ATTN_WIDTH = N_HEADS * V_DIM
CONV_WIDTH = 512
EPS = 1e-6

ROW_TILE = 512
IN_GROUP_ROWS = 256
OUT_ROW_TILE = 1024
OUT_GROUP_ROWS = 256
Q_TILE = 512
K_TILE = 512
POS_SPLIT_BITS = 6
SLOPE_PIECES = 3
SLOPE_ROWS = 16
LOG2E = 1.4426950408889634
ONES_ROWS = 16
NEG_BIG = -1e30
VMEM_LIMIT_BYTES = 48 * 1024 * 1024


def _silu(v):
    return v * jax.nn.sigmoid(v)


def _inproj_kernel(x_ref, g_ref, wn_ref, wqt_ref, wvt_ref,
                   kk_ref, ga_ref, cg_ref, u_ref, qt_ref, vt_ref, *, gate_scale):
    w = ATTN_WIDTH
    nt = (((1,), (1,)), ((), ()))
    groups = [slice(r0, r0 + IN_GROUP_ROWS) for r0 in range(0, x_ref.shape[0], IN_GROUP_ROWS)]
    hs = []
    for r in groups:
        x = x_ref[r, :]
        ms = jnp.mean(x * x, axis=-1, keepdims=True)
        hs.append((x * lax.rsqrt(ms + EPS) * g_ref[...]).astype(BF16))
    zns = [jnp.dot(h, wn_ref[...], preferred_element_type=F32) for h in hs]
    qts = [lax.dot_general(wqt_ref[...], h, nt, preferred_element_type=F32) for h in hs]
    vts = [lax.dot_general(wvt_ref[...], h, nt, preferred_element_type=F32) for h in hs]
    for r, zn, qt, vt in zip(groups, zns, qts, vts):
        kk_ref[r, :] = zn[:, 0:w].astype(BF16)
        ga_ref[r, :] = _silu(zn[:, w:2 * w]) * gate_scale
        cb = zn[:, 2 * w:3 * w]
        cc = zn[:, 3 * w:4 * w]
        ch = zn[:, 4 * w:5 * w]
        gc = zn[:, 5 * w:6 * w]
        cg_ref[r, :] = cb * _silu(gc)
        u_ref[r, :] = cc * ch
        qt_ref[0, :, r] = qt.astype(BF16)
        vt_ref[0, :, r] = vt.astype(BF16)


def _inproj(xf, g, wn, wqt, wvt, gate_scale):
    n, d = xf.shape
    tm = ROW_TILE
    nt = n // tm
    w = ATTN_WIDTH
    row_spec = lambda width: pl.BlockSpec((tm, width), lambda i: (i, 0))
    full = lambda a: pl.BlockSpec(a.shape, lambda i: (0,) * a.ndim)
    t_spec = pl.BlockSpec((1, w, tm), lambda i: (i, 0, 0))
    return pl.pallas_call(
        functools.partial(_inproj_kernel, gate_scale=gate_scale),
        grid=(nt,),
        in_specs=[row_spec(d), full(g), full(wn), full(wqt), full(wvt)],
        out_specs=[row_spec(w), row_spec(w), row_spec(w), row_spec(w), t_spec, t_spec],
        out_shape=[
            jax.ShapeDtypeStruct((n, w), BF16),
            jax.ShapeDtypeStruct((n, w), F32),
            jax.ShapeDtypeStruct((n, w), F32),
            jax.ShapeDtypeStruct((n, w), F32),
            jax.ShapeDtypeStruct((nt, w, tm), BF16),
            jax.ShapeDtypeStruct((nt, w, tm), BF16),
        ],
        compiler_params=pltpu.CompilerParams(
            dimension_semantics=("arbitrary",), vmem_limit_bytes=VMEM_LIMIT_BYTES),
        name="inproj",
    )(xf, g, wn, wqt, wvt)


LEFT, DIAG, RIGHT = 0, 1, 2
SIGMA = {LEFT: 1.0, DIAG: 0.0, RIGHT: -1.0}


def _attn_kernel(qt_ref, kk_ref, vt_ref, ga_ref, sg_ref, lq1_ref, lk1_ref, lq2_ref, lk2_ref,
                 o_ref, pos_ref, w_ref, dist_ref, bias_ref, s_ref, acc1_ref, acc2_ref, *, lambda_init):
    n_q = qt_ref.shape[1]
    n_k = vt_ref.shape[1]
    tq, tk = Q_TILE, K_TILE
    head = pl.program_id(1)

    slope_bits = (127 - 2 * (head + 1)) << 23
    slope = lax.bitcast_convert_type(jnp.full((1, 1), slope_bits, jnp.int32), F32)

    lam = (jnp.exp(jnp.sum(lq1_ref[...] * lk1_ref[...], axis=-1, keepdims=True))
           - jnp.exp(jnp.sum(lq2_ref[...] * lk2_ref[...], axis=-1, keepdims=True))
           + lambda_init)

    slope2 = slope * LOG2E

    @pl.when((pl.program_id(0) == 0) & (head == 0))
    def _init_head_independent_scratch():
        s_len = pos_ref.shape[0]
        kpos = lax.broadcasted_iota(jnp.int32, (s_len, 128), 0)
        lane = lax.broadcasted_iota(jnp.int32, (s_len, 128), 1)
        hi = ((kpos >> POS_SPLIT_BITS) << POS_SPLIT_BITS).astype(F32)
        lo = (kpos & ((1 << POS_SPLIT_BITS) - 1)).astype(F32)
        pos_ref[...] = jnp.where(lane < SLOPE_PIECES, hi,
                                 jnp.where(lane < 2 * SLOPE_PIECES, lo, 0.0)).astype(BF16)
        di = (lax.broadcasted_iota(jnp.int32, (tk, tq), 0)
              - lax.broadcasted_iota(jnp.int32, (tk, tq), 1))
        dist_ref[...] = jnp.abs(di).astype(F32)
        zeros = jnp.zeros((128 - SLOPE_ROWS, tq), BF16)
        for buf in range(2):
            for v in (LEFT, DIAG, RIGHT):
                for t in range(2):
                    w_ref[buf, v, t, 128 + SLOPE_ROWS:256, :] = zeros

    bias_ref[...] = -slope2 * dist_ref[...]

    ones_rows = jnp.where(lax.broadcasted_iota(jnp.int32, (ONES_ROWS, tk), 0) == 0, 1.0, 0.0).astype(BF16)
    row128 = lax.broadcasted_iota(jnp.int32, (128, tq), 0)
    qlane = lax.broadcasted_iota(jnp.int32, (1, tq), 1)

    neg_big = jnp.full((1, tq), NEG_BIG, F32)
    accs = (acc1_ref, acc2_ref)

    slope_row = lax.broadcasted_iota(jnp.int32, (SLOPE_ROWS, tq), 0)
    rest = jnp.broadcast_to(slope2, (SLOPE_ROWS, tq))
    pieces = []
    for _ in range(SLOPE_PIECES):
        piece = rest.astype(BF16).astype(F32)
        pieces.append(piece)
        rest = rest - piece
    piece_rows = pieces[SLOPE_PIECES - 1]
    for i in range(SLOPE_PIECES - 1):
        piece_rows = jnp.where((slope_row == i) | (slope_row == i + SLOPE_PIECES), pieces[i], piece_rows)
    for v in (LEFT, DIAG, RIGHT):
        rows = jnp.where(slope_row < 2 * SLOPE_PIECES, SIGMA[v] * piece_rows, 0.0).astype(BF16)
        for buf in range(2):
            for t in range(2):
                w_ref[buf, v, t, 128:128 + SLOPE_ROWS, :] = rows

    def build_weights(qi, buf):
        qq = qt_ref[0, qi].astype(F32)
        tops = (jnp.where(row128 < HEAD_DIM, qq, 0.0).astype(BF16),
                jnp.where(row128 >= HEAD_DIM, qq, 0.0).astype(BF16))
        for v in (LEFT, DIAG, RIGHT):
            for t in range(2):
                w_ref[buf, v, t, 0:128, :] = tops[t]

    def seq(i, qi):
        j = i - 1
        return jnp.where(i == 0, qi, j + jnp.where(j >= qi, 1, 0))

    def shift_of(c, qi):
        side = jnp.where(c < qi, -1.0, jnp.where(c > qi, 1.0, 0.0))
        return side * ((qlane + qi * tq).astype(F32) * slope2)

    def scores(c, qi, buf, slot, diag):
        k0 = c * tk if isinstance(c, int) else pl.multiple_of(c * tk, tk)
        lhs = jnp.concatenate([kk_ref[0, pl.ds(k0, tk), :], pos_ref[pl.ds(k0, tk), :]], axis=1)
        v = DIAG if diag else jnp.where(c < qi, LEFT, RIGHT)
        cms = []
        for t in range(2):
            s = jnp.dot(lhs, w_ref[buf, v, t], preferred_element_type=F32)
            if diag:
                s = s + bias_ref[...]
            s_ref[slot, t] = s
            cm = jnp.max(s, axis=0, keepdims=True)
            cms.append(cm if diag else cm + shift_of(c, qi))
        return tuple(cms)

    def softmax_pv(c, qi, slot, m_run, cms):
        vt = jnp.concatenate([vt_ref[0, c], ones_rows], axis=0)
        shift = shift_of(c, qi)
        new = []
        for t in range(2):
            m_new = jnp.maximum(m_run[t], cms[t])
            alpha = jnp.exp2(m_run[t] - m_new)
            p = jnp.exp2(s_ref[slot, t] - (m_new - shift)).astype(BF16)
            pv = jnp.dot(vt, p, preferred_element_type=F32)
            accs[t][...] = alpha * accs[t][...] + pv
            new.append(m_new)
        return tuple(new)

    def finalize(qi):
        q0 = pl.multiple_of(qi * tq, tq)
        a1 = acc1_ref[...]
        a2 = acc2_ref[...]
        att = a1[0:V_DIM] / a1[V_DIM:V_DIM + 1] - lam * (a2[0:V_DIM] / a2[V_DIM:V_DIM + 1])
        ms = jnp.mean(att * att, axis=0, keepdims=True)
        y = att * lax.rsqrt(ms + EPS) * sg_ref[...]
        o_ref[0, pl.ds(q0, tq), :] = (y.T * ga_ref[0, pl.ds(q0, tq), :]).astype(BF16)

    assert n_k % 2 == 0 and n_k >= 2

    def q_tile_body(qi, cms_diag):
        buf = qi % 2

        def pipe(jj, carry):
            m_run, cms = carry
            i = 2 * jj
            cms_odd = scores(seq(i + 1, qi), qi, buf, 1, False)
            m_run = softmax_pv(seq(i, qi), qi, 0, m_run, cms)
            cms_even = scores(seq(i + 2, qi), qi, buf, 0, False)
            m_run = softmax_pv(seq(i + 1, qi), qi, 1, m_run, cms_odd)
            return m_run, cms_even

        m_run, cms = lax.fori_loop(0, (n_k - 2) // 2, pipe, ((neg_big, neg_big), cms_diag), unroll=True)
        cms_last = scores(seq(n_k - 1, qi), qi, buf, 1, False)
        m_run = softmax_pv(seq(n_k - 2, qi), qi, 0, m_run, cms)
        q_next = jnp.minimum(qi + 1, n_q - 1)
        build_weights(q_next, 1 - buf)
        cms_diag_next = scores(q_next, q_next, 1 - buf, 0, True)
        softmax_pv(seq(n_k - 1, qi), qi, 1, m_run, cms_last)
        finalize(qi)
        return cms_diag_next

    acc1_ref[...] = jnp.zeros_like(acc1_ref)
    acc2_ref[...] = jnp.zeros_like(acc2_ref)
    build_weights(0, 0)
    lax.fori_loop(0, n_q, q_tile_body, scores(0, 0, 0, 0, True), unroll=2)


def _attention(qt, kk, vt, ga, sg, lq1, lk1, lq2, lk2, lambda_init):
    b, n_q, _, tq = qt.shape
    _, n_k, _, tk = vt.shape
    s = kk.shape[1]
    assert tq == Q_TILE and tk == K_TILE and n_q * tq == s and n_k * tk == s
    assert s <= 1 << (POS_SPLIT_BITS + 8)
    full = lambda a: pl.BlockSpec(a.shape, lambda bi, hi: (0,) * a.ndim)
    t_spec = lambda nblk, width: pl.BlockSpec((1, nblk, V_DIM, width), lambda bi, hi: (bi, 0, hi, 0))
    r_spec = pl.BlockSpec((1, s, V_DIM), lambda bi, hi: (bi, 0, hi))
    return pl.pallas_call(
        functools.partial(_attn_kernel, lambda_init=lambda_init),
        grid=(b, N_HEADS),
        in_specs=[t_spec(n_q, tq), r_spec, t_spec(n_k, tk), r_spec,
                  full(sg), full(lq1), full(lk1), full(lq2), full(lk2)],
        out_specs=r_spec,
        out_shape=jax.ShapeDtypeStruct((b, s, ATTN_WIDTH), BF16),
        scratch_shapes=[
            pltpu.VMEM((s, 128), BF16),
            pltpu.VMEM((2, 3, 2, 256, Q_TILE), BF16),
            pltpu.VMEM((K_TILE, Q_TILE), F32),
            pltpu.VMEM((K_TILE, Q_TILE), F32),
            pltpu.VMEM((2, 2, K_TILE, Q_TILE), F32),
            pltpu.VMEM((V_DIM + ONES_ROWS, Q_TILE), F32),
            pltpu.VMEM((V_DIM + ONES_ROWS, Q_TILE), F32),
        ],
        compiler_params=pltpu.CompilerParams(
            dimension_semantics=("arbitrary", "arbitrary"), vmem_limit_bytes=VMEM_LIMIT_BYTES),
        name="diff_attn",
    )(qt, kk, vt, ga, sg, lq1, lk1, lq2, lk2)


def _outproj_kernel(x_ref, oa_ref, cg_ref, u_ref, up_ref, un_ref, p_ref, cw_ref, wo32_ref,
                    g2_ref, wg32_ref, wp32_ref, gf_ref, out_ref, wo_ref, wg_ref, wp_ref,
                    *, tiles_per_seq, final_norm):
    i = pl.program_id(0)
    tm = x_ref.shape[0]

    @pl.when(i == 0)
    def _cast_weights_once():
        wo_ref[...] = wo32_ref[...].astype(BF16)
        wg_ref[...] = wg32_ref[...].astype(BF16)
        wp_ref[...] = wp32_ref[...].astype(BF16)

    u = u_ref[...]
    row = lax.broadcasted_iota(jnp.int32, (tm, 1), 0)
    t_in_seq = i % tiles_per_seq
    prev_row = jnp.where(t_in_seq == 0, 0.0, up_ref[7:8, :])
    next_row = jnp.where(t_in_seq == tiles_per_seq - 1, 0.0, un_ref[0:1, :])
    u_prev = jnp.where(row == 0, prev_row, pltpu.roll(u, 1, axis=0))
    u_next = jnp.where(row == tm - 1, next_row, pltpu.roll(u, tm - 1, axis=0))
    cw = cw_ref[...]
    conv = cw[0:1] * u_prev + cw[1:2] * u + cw[2:3] * u_next
    o_conv = (cg_ref[...] * conv).astype(BF16)
    aw = oa_ref.shape[1]
    groups = [slice(r0, r0 + OUT_GROUP_ROWS) for r0 in range(0, tm, OUT_GROUP_ROWS)]
    x1 = [x_ref[r, :]
          + jnp.dot(oa_ref[r, :], wo_ref[0:aw, :], preferred_element_type=F32)
          + jnp.dot(o_conv[r], wo_ref[aw:, :], preferred_element_type=F32) for r in groups]
    pp = [jnp.dot(p_ref[r, :].astype(BF16), wp_ref[...], preferred_element_type=F32) for r in groups]
    n1 = [(v * lax.rsqrt(jnp.mean(v * v, axis=-1, keepdims=True) + EPS) * g2_ref[...]).astype(BF16)
          for v in x1]
    gate = [jax.nn.sigmoid(jnp.dot(v, wg_ref[...], preferred_element_type=F32)) for v in n1]
    for r, a, g, q in zip(groups, x1, gate, pp):
        x2 = a + g * q
        if final_norm:
            ms2 = jnp.mean(x2 * x2, axis=-1, keepdims=True)
            x2 = x2 * lax.rsqrt(ms2 + EPS) * gf_ref[...]
        out_ref[r, :] = x2


def _outproj(xf, oa, cg, u, pf, cw, wo, g2, wg, wp, gf, seq_len, final_norm):
    n, d = xf.shape
    tm = OUT_ROW_TILE
    assert seq_len % tm == 0 and tm % OUT_GROUP_ROWS == 0
    nt = n // tm
    sub = 8
    nsub = n // sub
    row_spec = lambda width: pl.BlockSpec((tm, width), lambda i: (i, 0))
    full = lambda a: pl.BlockSpec(a.shape, lambda i: (0,) * a.ndim)
    prev_spec = pl.BlockSpec((sub, CONV_WIDTH), lambda i: (jnp.maximum(i * (tm // sub) - 1, 0), 0))
    next_spec = pl.BlockSpec((sub, CONV_WIDTH), lambda i: (jnp.minimum((i + 1) * (tm // sub), nsub - 1), 0))
    once = lambda a: pl.BlockSpec(a.shape, lambda i: (0,) * a.ndim, pipeline_mode=pl.Buffered(1))
    return pl.pallas_call(
        functools.partial(_outproj_kernel, tiles_per_seq=seq_len // tm, final_norm=final_norm),
        grid=(nt,),
        in_specs=[row_spec(d), row_spec(ATTN_WIDTH), row_spec(CONV_WIDTH), row_spec(CONV_WIDTH),
                  prev_spec, next_spec, row_spec(pf.shape[1]),
                  full(cw), once(wo), full(g2), once(wg), once(wp), full(gf)],
        out_specs=row_spec(d),
        out_shape=jax.ShapeDtypeStruct((n, d), F32),
        scratch_shapes=[pltpu.VMEM(wo.shape, BF16), pltpu.VMEM(wg.shape, BF16),
                        pltpu.VMEM(wp.shape, BF16)],
        compiler_params=pltpu.CompilerParams(
            dimension_semantics=("arbitrary",), vmem_limit_bytes=VMEM_LIMIT_BYTES),
        name="outproj",
    )(xf, oa, cg, u, u, u, pf, cw, wo, g2, wg, wp, gf)


Q_GROUP, V_GROUP = 0, 2


def _wprep_kernel(w_ref, wn_ref, wqt_ref, wvt_ref, *, q_scale):
    j = pl.program_id(0)

    @pl.when(j == Q_GROUP)
    def _():
        wqt_ref[...] = (w_ref[...] * q_scale).T.astype(BF16)

    @pl.when(j == V_GROUP)
    def _():
        wvt_ref[...] = w_ref[...].T.astype(BF16)

    @pl.when((j != Q_GROUP) & (j != V_GROUP))
    def _():
        wn_ref[...] = w_ref[...].astype(BF16)


def _wprep(wi, q_scale):
    d, n_cols = wi.shape
    w = ATTN_WIDTH
    n_groups = n_cols // w
    nat_block = lambda j: jnp.where(j <= Q_GROUP + 1, 0, jnp.where(j <= V_GROUP + 1, 1, j - 2))
    return pl.pallas_call(
        functools.partial(_wprep_kernel, q_scale=q_scale),
        grid=(n_groups,),
        in_specs=[pl.BlockSpec((d, w), lambda j: (0, j))],
        out_specs=[pl.BlockSpec((d, w), lambda j: (0, nat_block(j))),
                   pl.BlockSpec((w, d), lambda j: (0, 0)),
                   pl.BlockSpec((w, d), lambda j: (0, 0))],
        out_shape=[jax.ShapeDtypeStruct((d, n_cols - 2 * w), BF16),
                   jax.ShapeDtypeStruct((w, d), BF16),
                   jax.ShapeDtypeStruct((w, d), BF16)],
        compiler_params=pltpu.CompilerParams(
            dimension_semantics=("arbitrary",), vmem_limit_bytes=VMEM_LIMIT_BYTES),
        name="wprep",
    )(wi)


def kernel(x, p, mix_norm_g, w_in, lambda_q1, lambda_k1, lambda_q2, lambda_k2, subln_g, conv_w,
           w_out, ple_norm_g, w_ple_gate, w_ple_proj, final_norm_g):
    b, s, d = x.shape
    depth = w_in.shape[0]
    assert s % ROW_TILE == 0 and ROW_TILE == Q_TILE == K_TILE
    aw = ATTN_WIDTH
    xf = x.reshape(b * s, d)
    gf = final_norm_g.reshape(1, d)
    for i in range(depth):
        lambda_init = 0.8 - 0.6 * math.exp(-0.3 * i)
        wn, wqt, wvt = _wprep(w_in[i], HEAD_DIM ** -0.5 * LOG2E)
        kk, ga, cg, u, qt, vt = _inproj(xf, mix_norm_g[i].reshape(1, d), wn, wqt, wvt,
                                        1.0 - lambda_init)
        n_t = s // ROW_TILE
        oa = _attention(qt.reshape(b, n_t, aw, ROW_TILE), kk.reshape(b, s, aw),
                        vt.reshape(b, n_t, aw, ROW_TILE), ga.reshape(b, s, aw),
                        subln_g[i].reshape(V_DIM, 1),
                        lambda_q1[i].reshape(1, -1), lambda_k1[i].reshape(1, -1),
                        lambda_q2[i].reshape(1, -1), lambda_k2[i].reshape(1, -1), lambda_init)
        xf = _outproj(xf, oa.reshape(b * s, aw), cg, u, p[i].reshape(b * s, -1), conv_w[i],
                      w_out[i], ple_norm_g[i].reshape(1, d), w_ple_gate[i], w_ple_proj[i], gf, s,
                      final_norm=(i == depth - 1))
    return xf.reshape(b, s, d)
```

```python
import functools
import math

import jax
import jax.numpy as jnp
from jax import lax
from jax.experimental import pallas as pl
from jax.experimental.pallas import tpu as pltpu

F32 = jnp.float32
BF16 = jnp.bfloat16

N_HEADS = 4
HEAD_DIM = 64
V_DIM = 2 * HEAD_DIM
ATTN_WIDTH = N_HEADS * V_DIM
CONV_WIDTH = 512
EPS = 1e-6
LANES = 128

ROW_TILE = 512
IN_GROUP_ROWS = 256
OUT_ROW_TILE = 1024
OUT_GROUP_ROWS = 256
Q_TILE = 512
K_TILE = 512
POS_SPLIT_BITS = 6
SLOPE_PIECES = 3
SLOPE_ROWS = 16
LOG2E = 1.4426950408889634
ONES_ROWS = 16
NEG_BIG = -1e30
VMEM_LIMIT_BYTES = 48 * 1024 * 1024


def _silu(v):
    return v * jax.nn.sigmoid(v)


def _inproj_kernel(x_ref, g_ref, wn_ref, wqt_ref, wvt_ref,
                   kk_ref, ga_ref, cg_ref, u_ref, qt_ref, vt_ref, *, gate_scale):
    w = ATTN_WIDTH
    nt = (((1,), (1,)), ((), ()))
    groups = [slice(r0, r0 + IN_GROUP_ROWS) for r0 in range(0, x_ref.shape[0], IN_GROUP_ROWS)]
    hs = []
    for r in groups:
        x = x_ref[r, :]
        ms = jnp.mean(x * x, axis=-1, keepdims=True)
        hs.append((x * lax.rsqrt(ms + EPS) * g_ref[...]).astype(BF16))
    zns = [jnp.dot(h, wn_ref[...], preferred_element_type=F32) for h in hs]
    qts = [lax.dot_general(wqt_ref[...], h, nt, preferred_element_type=F32) for h in hs]
    vts = [lax.dot_general(wvt_ref[...], h, nt, preferred_element_type=F32) for h in hs]
    for r, zn, qt, vt in zip(groups, zns, qts, vts):
        kk_ref[r, :] = zn[:, 0:w].astype(BF16)
        ga_ref[r, :] = _silu(zn[:, w:2 * w]) * gate_scale
        cb = zn[:, 2 * w:3 * w]
        cc = zn[:, 3 * w:4 * w]
        ch = zn[:, 4 * w:5 * w]
        gc = zn[:, 5 * w:6 * w]
        cg_ref[r, :] = cb * _silu(gc)
        u_ref[r, :] = cc * ch
        qt_ref[0, :, r] = qt.astype(BF16)
        vt_ref[0, :, r] = vt.astype(BF16)


def _inproj(xf, g, wn, wqt, wvt, gate_scale):
    n, d = xf.shape
    tm = ROW_TILE
    nt = n // tm
    w = ATTN_WIDTH
    row_spec = lambda width: pl.BlockSpec((tm, width), lambda i: (i, 0))
    full = lambda a: pl.BlockSpec(a.shape, lambda i: (0,) * a.ndim)
    t_spec = pl.BlockSpec((1, w, tm), lambda i: (i, 0, 0))
    return pl.pallas_call(
        functools.partial(_inproj_kernel, gate_scale=gate_scale),
        grid=(nt,),
        in_specs=[row_spec(d), full(g), full(wn), full(wqt), full(wvt)],
        out_specs=[row_spec(w), row_spec(w), row_spec(w), row_spec(w), t_spec, t_spec],
        out_shape=[
            jax.ShapeDtypeStruct((n, w), BF16),
            jax.ShapeDtypeStruct((n, w), F32),
            jax.ShapeDtypeStruct((n, w), F32),
            jax.ShapeDtypeStruct((n, w), F32),
            jax.ShapeDtypeStruct((nt, w, tm), BF16),
            jax.ShapeDtypeStruct((nt, w, tm), BF16),
        ],
        compiler_params=pltpu.CompilerParams(
            dimension_semantics=("arbitrary",), vmem_limit_bytes=VMEM_LIMIT_BYTES),
        name="inproj",
    )(xf, g, wn, wqt, wvt)


LEFT, DIAG, RIGHT = 0, 1, 2
SIGMA = {LEFT: 1.0, DIAG: 0.0, RIGHT: -1.0}


def _attn_kernel(qt_ref, kk_ref, vt_ref, ga_ref, sg_ref, lq1_ref, lk1_ref, lq2_ref, lk2_ref,
                 o_ref, pos_ref, w_ref, dist_ref, bias_ref, s_ref, acc1_ref, acc2_ref, *, lambda_init):
    n_q = qt_ref.shape[1]
    n_k = vt_ref.shape[1]
    tq, tk = Q_TILE, K_TILE
    head = pl.program_id(1)

    slope_bits = (127 - 2 * (head + 1)) << 23
    slope = lax.bitcast_convert_type(jnp.full((1, 1), slope_bits, jnp.int32), F32)

    lam = (jnp.exp(jnp.sum(lq1_ref[...] * lk1_ref[...], axis=-1, keepdims=True))
           - jnp.exp(jnp.sum(lq2_ref[...] * lk2_ref[...], axis=-1, keepdims=True))
           + lambda_init)

    slope2 = slope * LOG2E

    @pl.when((pl.program_id(0) == 0) & (head == 0))
    def _init_head_independent_scratch():
        s_len = pos_ref.shape[0]
        kpos = lax.broadcasted_iota(jnp.int32, (s_len, LANES), 0)
        lane = lax.broadcasted_iota(jnp.int32, (s_len, LANES), 1)
        hi = ((kpos >> POS_SPLIT_BITS) << POS_SPLIT_BITS).astype(F32)
        lo = (kpos & ((1 << POS_SPLIT_BITS) - 1)).astype(F32)
        pos_ref[...] = jnp.where(lane < SLOPE_PIECES, hi,
                                 jnp.where(lane < 2 * SLOPE_PIECES, lo, 0.0)).astype(BF16)
        di = (lax.broadcasted_iota(jnp.int32, (tk, tq), 0)
              - lax.broadcasted_iota(jnp.int32, (tk, tq), 1))
        dist_ref[...] = jnp.abs(di).astype(F32)
        zeros = jnp.zeros((LANES - SLOPE_ROWS, tq), BF16)
        for buf in range(2):
            for v in (LEFT, DIAG, RIGHT):
                for t in range(2):
                    w_ref[buf, v, t, V_DIM + SLOPE_ROWS:V_DIM + LANES, :] = zeros

    bias_ref[...] = -slope2 * dist_ref[...]

    ones_rows = jnp.where(lax.broadcasted_iota(jnp.int32, (ONES_ROWS, tk), 0) == 0, 1.0, 0.0).astype(BF16)
    q_row = lax.broadcasted_iota(jnp.int32, (V_DIM, tq), 0)
    qlane = lax.broadcasted_iota(jnp.int32, (1, tq), 1)

    neg_big = jnp.full((1, tq), NEG_BIG, F32)
    accs = (acc1_ref, acc2_ref)

    slope_row = lax.broadcasted_iota(jnp.int32, (SLOPE_ROWS, tq), 0)
    rest = jnp.broadcast_to(slope2, (SLOPE_ROWS, tq))
    pieces = []
    for _ in range(SLOPE_PIECES):
        piece = rest.astype(BF16).astype(F32)
        pieces.append(piece)
        rest = rest - piece
    piece_rows = pieces[SLOPE_PIECES - 1]
    for i in range(SLOPE_PIECES - 1):
        piece_rows = jnp.where((slope_row == i) | (slope_row == i + SLOPE_PIECES), pieces[i], piece_rows)
    for v in (LEFT, DIAG, RIGHT):
        rows = jnp.where(slope_row < 2 * SLOPE_PIECES, SIGMA[v] * piece_rows, 0.0).astype(BF16)
        for buf in range(2):
            for t in range(2):
                w_ref[buf, v, t, V_DIM:V_DIM + SLOPE_ROWS, :] = rows

    def build_weights(qi, buf):
        qq = qt_ref[0, qi].astype(F32)
        tops = (jnp.where(q_row < HEAD_DIM, qq, 0.0).astype(BF16),
                jnp.where(q_row >= HEAD_DIM, qq, 0.0).astype(BF16))
        for v in (LEFT, DIAG, RIGHT):
            for t in range(2):
                w_ref[buf, v, t, 0:V_DIM, :] = tops[t]

    def seq(i, qi):
        j = i - 1
        return jnp.where(i == 0, qi, j + jnp.where(j >= qi, 1, 0))

    def shift_of(c, qi):
        side = jnp.where(c < qi, -1.0, jnp.where(c > qi, 1.0, 0.0))
        return side * ((qlane + qi * tq).astype(F32) * slope2)

    def scores(c, qi, buf, slot, diag):
        k0 = c * tk if isinstance(c, int) else pl.multiple_of(c * tk, tk)
        lhs = jnp.concatenate([kk_ref[0, pl.ds(k0, tk), :], pos_ref[pl.ds(k0, tk), :]], axis=1)
        v = DIAG if diag else jnp.where(c < qi, LEFT, RIGHT)
        cms = []
        for t in range(2):
            s = jnp.dot(lhs, w_ref[buf, v, t], preferred_element_type=F32)
            if diag:
                s = s + bias_ref[...]
            s_ref[slot, t] = s
            cm = jnp.max(s, axis=0, keepdims=True)
            cms.append(cm if diag else cm + shift_of(c, qi))
        return tuple(cms)

    def softmax_pv(c, qi, slot, m_run, cms):
        vt = jnp.concatenate([vt_ref[0, c], ones_rows], axis=0)
        shift = shift_of(c, qi)
        new = []
        for t in range(2):
            m_new = jnp.maximum(m_run[t], cms[t])
            alpha = jnp.exp2(m_run[t] - m_new)
            p = jnp.exp2(s_ref[slot, t] - (m_new - shift)).astype(BF16)
            pv = jnp.dot(vt, p, preferred_element_type=F32)
            accs[t][...] = alpha * accs[t][...] + pv
            new.append(m_new)
        return tuple(new)

    def finalize(qi):
        q0 = pl.multiple_of(qi * tq, tq)
        a1 = acc1_ref[...]
        a2 = acc2_ref[...]
        att = a1[0:V_DIM] / a1[V_DIM:V_DIM + 1] - lam * (a2[0:V_DIM] / a2[V_DIM:V_DIM + 1])
        ms = jnp.mean(att * att, axis=0, keepdims=True)
        y = att * lax.rsqrt(ms + EPS) * sg_ref[...]
        o_ref[0, pl.ds(q0, tq), :] = (y.T * ga_ref[0, pl.ds(q0, tq), :]).astype(BF16)

    assert n_k % 2 == 0 and n_k >= 2

    def q_tile_body(qi, cms_diag):
        buf = qi % 2

        def pipe(jj, carry):
            m_run, cms = carry
            i = 2 * jj
            cms_odd = scores(seq(i + 1, qi), qi, buf, 1, False)
            m_run = softmax_pv(seq(i, qi), qi, 0, m_run, cms)
            cms_even = scores(seq(i + 2, qi), qi, buf, 0, False)
            m_run = softmax_pv(seq(i + 1, qi), qi, 1, m_run, cms_odd)
            return m_run, cms_even

        m_run, cms = lax.fori_loop(0, (n_k - 2) // 2, pipe, ((neg_big, neg_big), cms_diag), unroll=True)
        cms_last = scores(seq(n_k - 1, qi), qi, buf, 1, False)
        m_run = softmax_pv(seq(n_k - 2, qi), qi, 0, m_run, cms)
        q_next = jnp.minimum(qi + 1, n_q - 1)
        build_weights(q_next, 1 - buf)
        cms_diag_next = scores(q_next, q_next, 1 - buf, 0, True)
        softmax_pv(seq(n_k - 1, qi), qi, 1, m_run, cms_last)
        finalize(qi)
        return cms_diag_next

    acc1_ref[...] = jnp.zeros_like(acc1_ref)
    acc2_ref[...] = jnp.zeros_like(acc2_ref)
    build_weights(0, 0)
    lax.fori_loop(0, n_q, q_tile_body, scores(0, 0, 0, 0, True), unroll=4)


def _attention(qt, kk, vt, ga, sg, lq1, lk1, lq2, lk2, lambda_init):
    b, n_q, _, tq = qt.shape
    _, n_k, _, tk = vt.shape
    s = kk.shape[1]
    assert tq == Q_TILE and tk == K_TILE and n_q * tq == s and n_k * tk == s
    assert s <= 1 << (POS_SPLIT_BITS + 8)
    full = lambda a: pl.BlockSpec(a.shape, lambda bi, hi: (0,) * a.ndim)
    t_spec = lambda nblk, width: pl.BlockSpec((1, nblk, V_DIM, width), lambda bi, hi: (bi, 0, hi, 0))
    r_spec = pl.BlockSpec((1, s, V_DIM), lambda bi, hi: (bi, 0, hi))
    return pl.pallas_call(
        functools.partial(_attn_kernel, lambda_init=lambda_init),
        grid=(b, N_HEADS),
        in_specs=[t_spec(n_q, tq), r_spec, t_spec(n_k, tk), r_spec,
                  full(sg), full(lq1), full(lk1), full(lq2), full(lk2)],
        out_specs=r_spec,
        out_shape=jax.ShapeDtypeStruct((b, s, ATTN_WIDTH), BF16),
        scratch_shapes=[
            pltpu.VMEM((s, LANES), BF16),
            pltpu.VMEM((2, 3, 2, V_DIM + LANES, Q_TILE), BF16),
            pltpu.VMEM((K_TILE, Q_TILE), F32),
            pltpu.VMEM((K_TILE, Q_TILE), F32),
            pltpu.VMEM((2, 2, K_TILE, Q_TILE), F32),
            pltpu.VMEM((V_DIM + ONES_ROWS, Q_TILE), F32),
            pltpu.VMEM((V_DIM + ONES_ROWS, Q_TILE), F32),
        ],
        compiler_params=pltpu.CompilerParams(
            dimension_semantics=("arbitrary", "arbitrary"), vmem_limit_bytes=VMEM_LIMIT_BYTES),
        name="diff_attn",
    )(qt, kk, vt, ga, sg, lq1, lk1, lq2, lk2)


def _outproj_kernel(x_ref, oa_ref, cg_ref, u_ref, up_ref, un_ref, p_ref, cw_ref, wo32_ref,
                    g2_ref, wg32_ref, wp32_ref, gf_ref, out_ref, wo_ref, wg_ref, wp_ref,
                    *, tiles_per_seq, final_norm):
    i = pl.program_id(0)
    tm = x_ref.shape[0]

    @pl.when(i == 0)
    def _cast_weights_once():
        wo_ref[...] = wo32_ref[...].astype(BF16)
        wg_ref[...] = wg32_ref[...].astype(BF16)
        wp_ref[...] = wp32_ref[...].astype(BF16)

    u = u_ref[...]
    row = lax.broadcasted_iota(jnp.int32, (tm, 1), 0)
    t_in_seq = i % tiles_per_seq
    prev_row = jnp.where(t_in_seq == 0, 0.0, up_ref[7:8, :])
    next_row = jnp.where(t_in_seq == tiles_per_seq - 1, 0.0, un_ref[0:1, :])
    u_prev = jnp.where(row == 0, prev_row, pltpu.roll(u, 1, axis=0))
    u_next = jnp.where(row == tm - 1, next_row, pltpu.roll(u, tm - 1, axis=0))
    cw = cw_ref[...]
    conv = cw[0:1] * u_prev + cw[1:2] * u + cw[2:3] * u_next
    o_conv = (cg_ref[...] * conv).astype(BF16)
    aw = oa_ref.shape[1]
    groups = [slice(r0, r0 + OUT_GROUP_ROWS) for r0 in range(0, tm, OUT_GROUP_ROWS)]
    x1 = [x_ref[r, :]
          + jnp.dot(oa_ref[r, :], wo_ref[0:aw, :], preferred_element_type=F32)
          + jnp.dot(o_conv[r], wo_ref[aw:, :], preferred_element_type=F32) for r in groups]
    pp = [jnp.dot(p_ref[r, :].astype(BF16), wp_ref[...], preferred_element_type=F32) for r in groups]
    n1 = [(v * lax.rsqrt(jnp.mean(v * v, axis=-1, keepdims=True) + EPS) * g2_ref[...]).astype(BF16)
          for v in x1]
    gate = [jax.nn.sigmoid(jnp.dot(v, wg_ref[...], preferred_element_type=F32)) for v in n1]
    for r, a, g, q in zip(groups, x1, gate, pp):
        x2 = a + g * q
        if final_norm:
            ms2 = jnp.mean(x2 * x2, axis=-1, keepdims=True)
            x2 = x2 * lax.rsqrt(ms2 + EPS) * gf_ref[...]
        out_ref[r, :] = x2


def _outproj(xf, oa, cg, u, pf, cw, wo, g2, wg, wp, gf, seq_len, final_norm):
    n, d = xf.shape
    tm = OUT_ROW_TILE
    assert seq_len % tm == 0 and tm % OUT_GROUP_ROWS == 0
    nt = n // tm
    sub = 8
    nsub = n // sub
    row_spec = lambda width: pl.BlockSpec((tm, width), lambda i: (i, 0))
    full = lambda a: pl.BlockSpec(a.shape, lambda i: (0,) * a.ndim)
    prev_spec = pl.BlockSpec((sub, CONV_WIDTH), lambda i: (jnp.maximum(i * (tm // sub) - 1, 0), 0))
    next_spec = pl.BlockSpec((sub, CONV_WIDTH), lambda i: (jnp.minimum((i + 1) * (tm // sub), nsub - 1), 0))
    once = lambda a: pl.BlockSpec(a.shape, lambda i: (0,) * a.ndim, pipeline_mode=pl.Buffered(1))
    return pl.pallas_call(
        functools.partial(_outproj_kernel, tiles_per_seq=seq_len // tm, final_norm=final_norm),
        grid=(nt,),
        in_specs=[row_spec(d), row_spec(ATTN_WIDTH), row_spec(CONV_WIDTH), row_spec(CONV_WIDTH),
                  prev_spec, next_spec, row_spec(pf.shape[1]),
                  full(cw), once(wo), full(g2), once(wg), once(wp), full(gf)],
        out_specs=row_spec(d),
        out_shape=jax.ShapeDtypeStruct((n, d), F32),
        scratch_shapes=[pltpu.VMEM(wo.shape, BF16), pltpu.VMEM(wg.shape, BF16),
                        pltpu.VMEM(wp.shape, BF16)],
        compiler_params=pltpu.CompilerParams(
            dimension_semantics=("arbitrary",), vmem_limit_bytes=VMEM_LIMIT_BYTES),
        name="outproj",
    )(xf, oa, cg, u, u, u, pf, cw, wo, g2, wg, wp, gf)


Q_GROUP, V_GROUP = 0, 2


def _wprep_kernel(w_ref, wn_ref, wqt_ref, wvt_ref, *, q_scale):
    j = pl.program_id(0)

    @pl.when(j == Q_GROUP)
    def _():
        wqt_ref[...] = (w_ref[...] * q_scale).T.astype(BF16)

    @pl.when(j == V_GROUP)
    def _():
        wvt_ref[...] = w_ref[...].T.astype(BF16)

    @pl.when((j != Q_GROUP) & (j != V_GROUP))
    def _():
        wn_ref[...] = w_ref[...].astype(BF16)


def _wprep(wi, q_scale):
    d, n_cols = wi.shape
    w = ATTN_WIDTH
    n_groups = n_cols // w
    nat_block = lambda j: jnp.where(j <= Q_GROUP + 1, 0, jnp.where(j <= V_GROUP + 1, 1, j - 2))
    return pl.pallas_call(
        functools.partial(_wprep_kernel, q_scale=q_scale),
        grid=(n_groups,),
        in_specs=[pl.BlockSpec((d, w), lambda j: (0, j))],
        out_specs=[pl.BlockSpec((d, w), lambda j: (0, nat_block(j))),
                   pl.BlockSpec((w, d), lambda j: (0, 0)),
                   pl.BlockSpec((w, d), lambda j: (0, 0))],
        out_shape=[jax.ShapeDtypeStruct((d, n_cols - 2 * w), BF16),
                   jax.ShapeDtypeStruct((w, d), BF16),
                   jax.ShapeDtypeStruct((w, d), BF16)],
        compiler_params=pltpu.CompilerParams(
            dimension_semantics=("arbitrary",), vmem_limit_bytes=VMEM_LIMIT_BYTES),
        name="wprep",
    )(wi)


def kernel(x, p, mix_norm_g, w_in, lambda_q1, lambda_k1, lambda_q2, lambda_k2, subln_g, conv_w,
           w_out, ple_norm_g, w_ple_gate, w_ple_proj, final_norm_g):
    b, s, d = x.shape
    depth = w_in.shape[0]
    assert s % ROW_TILE == 0 and ROW_TILE == Q_TILE == K_TILE
    aw = ATTN_WIDTH
    xf = x.reshape(b * s, d)
    gf = final_norm_g.reshape(1, d)
    for i in range(depth):
        lambda_init = 0.8 - 0.6 * math.exp(-0.3 * i)
        wn, wqt, wvt = _wprep(w_in[i], HEAD_DIM ** -0.5 * LOG2E)
        kk, ga, cg, u, qt, vt = _inproj(xf, mix_norm_g[i].reshape(1, d), wn, wqt, wvt,
                                        1.0 - lambda_init)
        n_t = s // ROW_TILE
        oa = _attention(qt.reshape(b, n_t, aw, ROW_TILE), kk.reshape(b, s, aw),
                        vt.reshape(b, n_t, aw, ROW_TILE), ga.reshape(b, s, aw),
                        subln_g[i].reshape(V_DIM, 1),
                        lambda_q1[i].reshape(1, -1), lambda_k1[i].reshape(1, -1),
                        lambda_q2[i].reshape(1, -1), lambda_k2[i].reshape(1, -1), lambda_init)
        xf = _outproj(xf, oa.reshape(b * s, aw), cg, u, p[i].reshape(b * s, -1), conv_w[i],
                      w_out[i], ple_norm_g[i].reshape(1, d), w_ple_gate[i], w_ple_proj[i], gf, s,
                      final_norm=(i == depth - 1))
    return xf.reshape(b, s, d)
```

```python
import functools
import math

import jax
import jax.numpy as jnp
from jax import lax
from jax.experimental import pallas as pl
from jax.experimental.pallas import tpu as pltpu

F32 = jnp.float32
BF16 = jnp.bfloat16

N_HEADS = 4
HEAD_DIM = 64
V_DIM = 2 * HEAD_DIM
ATTN_WIDTH = N_HEADS * V_DIM
CONV_WIDTH = 512
EPS = 1e-6
LANES = 128

ROW_TILE = 512
IN_GROUP_ROWS = 256
OUT_ROW_TILE = 1024
OUT_GROUP_ROWS = 256
Q_TILE = 512
K_TILE = 512
POS_SPLIT_BITS = 6
SLOPE_PIECES = 3
SLOPE_ROWS = 16
LOG2E = 1.4426950408889634
ONES_ROWS = 16
NEG_BIG = -1e30
VMEM_LIMIT_BYTES = 48 * 1024 * 1024


def _silu(v):
    return v * jax.nn.sigmoid(v)


Q_GROUP, V_GROUP = 0, 2


def _inproj_kernel(x_ref, g_ref, w32_ref, kk_ref, ga_ref, cg_ref, u_ref, qt_ref, vt_ref,
                   wn_ref, wqt_ref, wvt_ref, *, gate_scale, q_scale):
    w = ATTN_WIDTH
    nt = (((1,), (1,)), ((), ()))

    @pl.when(pl.program_id(0) == 0)
    def _relayout_weights_once():
        col = 0
        for grp in range(w32_ref.shape[1] // w):
            blk = w32_ref[:, grp * w:(grp + 1) * w]
            if grp == Q_GROUP:
                wqt_ref[...] = (blk * q_scale).T.astype(BF16)
            elif grp == V_GROUP:
                wvt_ref[...] = blk.T.astype(BF16)
            else:
                wn_ref[:, col:col + w] = blk.astype(BF16)
                col += w

    groups = [slice(r0, r0 + IN_GROUP_ROWS) for r0 in range(0, x_ref.shape[0], IN_GROUP_ROWS)]
    hs = []
    for r in groups:
        x = x_ref[r, :]
        ms = jnp.mean(x * x, axis=-1, keepdims=True)
        hs.append((x * lax.rsqrt(ms + EPS) * g_ref[...]).astype(BF16))
    zns = [jnp.dot(h, wn_ref[...], preferred_element_type=F32) for h in hs]
    qts = [lax.dot_general(wqt_ref[...], h, nt, preferred_element_type=F32) for h in hs]
    vts = [lax.dot_general(wvt_ref[...], h, nt, preferred_element_type=F32) for h in hs]
    for r, zn, qt, vt in zip(groups, zns, qts, vts):
        kk_ref[r, :] = zn[:, 0:w].astype(BF16)
        ga_ref[r, :] = _silu(zn[:, w:2 * w]) * gate_scale
        cb = zn[:, 2 * w:3 * w]
        cc = zn[:, 3 * w:4 * w]
        ch = zn[:, 4 * w:5 * w]
        gc = zn[:, 5 * w:6 * w]
        cg_ref[r, :] = cb * _silu(gc)
        u_ref[r, :] = cc * ch
        qt_ref[0, :, r] = qt.astype(BF16)
        vt_ref[0, :, r] = vt.astype(BF16)


def _inproj(xf, g, wi, gate_scale, q_scale):
    n, d = xf.shape
    tm = ROW_TILE
    nt = n // tm
    w = ATTN_WIDTH
    n_cols = wi.shape[1]
    row_spec = lambda width: pl.BlockSpec((tm, width), lambda i: (i, 0))
    full = lambda a: pl.BlockSpec(a.shape, lambda i: (0,) * a.ndim)
    once = lambda a: pl.BlockSpec(a.shape, lambda i: (0,) * a.ndim, pipeline_mode=pl.Buffered(1))
    t_spec = pl.BlockSpec((1, w, tm), lambda i: (i, 0, 0))
    return pl.pallas_call(
        functools.partial(_inproj_kernel, gate_scale=gate_scale, q_scale=q_scale),
        grid=(nt,),
        in_specs=[row_spec(d), full(g), once(wi)],
        out_specs=[row_spec(w), row_spec(w), row_spec(w), row_spec(w), t_spec, t_spec],
        out_shape=[
            jax.ShapeDtypeStruct((n, w), BF16),
            jax.ShapeDtypeStruct((n, w), F32),
            jax.ShapeDtypeStruct((n, w), F32),
            jax.ShapeDtypeStruct((n, w), F32),
            jax.ShapeDtypeStruct((nt, w, tm), BF16),
            jax.ShapeDtypeStruct((nt, w, tm), BF16),
        ],
        scratch_shapes=[pltpu.VMEM((d, n_cols - 2 * w), BF16),
                        pltpu.VMEM((w, d), BF16),
                        pltpu.VMEM((w, d), BF16)],
        compiler_params=pltpu.CompilerParams(
            dimension_semantics=("arbitrary",), vmem_limit_bytes=VMEM_LIMIT_BYTES),
        name="inproj",
    )(xf, g, wi)


LEFT, DIAG, RIGHT = 0, 1, 2
SIGMA = {LEFT: 1.0, DIAG: 0.0, RIGHT: -1.0}


def _attn_kernel(qt_ref, kk_ref, vt_ref, ga_ref, sg_ref, lq1_ref, lk1_ref, lq2_ref, lk2_ref,
                 o_ref, pos_ref, w_ref, dist_ref, bias_ref, s_ref, acc1_ref, acc2_ref, *, lambda_init):
    n_q = qt_ref.shape[1]
    n_k = vt_ref.shape[1]
    tq, tk = Q_TILE, K_TILE
    head = pl.program_id(1)

    slope_bits = (127 - 2 * (head + 1)) << 23
    slope = lax.bitcast_convert_type(jnp.full((1, 1), slope_bits, jnp.int32), F32)

    lam = (jnp.exp(jnp.sum(lq1_ref[...] * lk1_ref[...], axis=-1, keepdims=True))
           - jnp.exp(jnp.sum(lq2_ref[...] * lk2_ref[...], axis=-1, keepdims=True))
           + lambda_init)

    slope2 = slope * LOG2E

    @pl.when((pl.program_id(0) == 0) & (head == 0))
    def _init_head_independent_scratch():
        s_len = pos_ref.shape[0]
        kpos = lax.broadcasted_iota(jnp.int32, (s_len, LANES), 0)
        lane = lax.broadcasted_iota(jnp.int32, (s_len, LANES), 1)
        hi = ((kpos >> POS_SPLIT_BITS) << POS_SPLIT_BITS).astype(F32)
        lo = (kpos & ((1 << POS_SPLIT_BITS) - 1)).astype(F32)
        pos_ref[...] = jnp.where(lane < SLOPE_PIECES, hi,
                                 jnp.where(lane < 2 * SLOPE_PIECES, lo, 0.0)).astype(BF16)
        di = (lax.broadcasted_iota(jnp.int32, (tk, tq), 0)
              - lax.broadcasted_iota(jnp.int32, (tk, tq), 1))
        dist_ref[...] = jnp.abs(di).astype(F32)
        zeros = jnp.zeros((LANES - SLOPE_ROWS, tq), BF16)
        for buf in range(2):
            for v in (LEFT, DIAG, RIGHT):
                for t in range(2):
                    w_ref[buf, v, t, V_DIM + SLOPE_ROWS:V_DIM + LANES, :] = zeros

    bias_ref[...] = -slope2 * dist_ref[...]

    ones_rows = jnp.where(lax.broadcasted_iota(jnp.int32, (ONES_ROWS, tk), 0) == 0, 1.0, 0.0).astype(BF16)
    q_row = lax.broadcasted_iota(jnp.int32, (V_DIM, tq), 0)
    qlane = lax.broadcasted_iota(jnp.int32, (1, tq), 1)

    neg_big = jnp.full((1, tq), NEG_BIG, F32)
    accs = (acc1_ref, acc2_ref)

    slope_row = lax.broadcasted_iota(jnp.int32, (SLOPE_ROWS, tq), 0)
    rest = jnp.broadcast_to(slope2, (SLOPE_ROWS, tq))
    pieces = []
    for _ in range(SLOPE_PIECES):
        piece = rest.astype(BF16).astype(F32)
        pieces.append(piece)
        rest = rest - piece
    piece_rows = pieces[SLOPE_PIECES - 1]
    for i in range(SLOPE_PIECES - 1):
        piece_rows = jnp.where((slope_row == i) | (slope_row == i + SLOPE_PIECES), pieces[i], piece_rows)
    for v in (LEFT, DIAG, RIGHT):
        rows = jnp.where(slope_row < 2 * SLOPE_PIECES, SIGMA[v] * piece_rows, 0.0).astype(BF16)
        for buf in range(2):
            for t in range(2):
                w_ref[buf, v, t, V_DIM:V_DIM + SLOPE_ROWS, :] = rows

    def build_weights(qi, buf):
        qq = qt_ref[0, qi].astype(F32)
        tops = (jnp.where(q_row < HEAD_DIM, qq, 0.0).astype(BF16),
                jnp.where(q_row >= HEAD_DIM, qq, 0.0).astype(BF16))
        for v in (LEFT, DIAG, RIGHT):
            for t in range(2):
                w_ref[buf, v, t, 0:V_DIM, :] = tops[t]

    def seq(i, qi):
        j = i - 1
        return jnp.where(i == 0, qi, j + jnp.where(j >= qi, 1, 0))

    def shift_of(c, qi):
        side = jnp.where(c < qi, -1.0, jnp.where(c > qi, 1.0, 0.0))
        return side * ((qlane + qi * tq).astype(F32) * slope2)

    def scores(c, qi, buf, slot, diag):
        k0 = c * tk if isinstance(c, int) else pl.multiple_of(c * tk, tk)
        lhs = jnp.concatenate([kk_ref[0, pl.ds(k0, tk), :], pos_ref[pl.ds(k0, tk), :]], axis=1)
        v = DIAG if diag else jnp.where(c < qi, LEFT, RIGHT)
        cms = []
        for t in range(2):
            s = jnp.dot(lhs, w_ref[buf, v, t], preferred_element_type=F32)
            if diag:
                s = s + bias_ref[...]
            s_ref[slot, t] = s
            cm = jnp.max(s, axis=0, keepdims=True)
            cms.append(cm if diag else cm + shift_of(c, qi))
        return tuple(cms)

    def softmax_pv(c, qi, slot, m_run, cms):
        vt = jnp.concatenate([vt_ref[0, c], ones_rows], axis=0)
        shift = shift_of(c, qi)
        new = []
        for t in range(2):
            m_new = jnp.maximum(m_run[t], cms[t])
            alpha = jnp.exp2(m_run[t] - m_new)
            p = jnp.exp2(s_ref[slot, t] - (m_new - shift)).astype(BF16)
            pv = jnp.dot(vt, p, preferred_element_type=F32)
            accs[t][...] = alpha * accs[t][...] + pv
            new.append(m_new)
        return tuple(new)

    def finalize(qi):
        q0 = pl.multiple_of(qi * tq, tq)
        a1 = acc1_ref[...]
        a2 = acc2_ref[...]
        att = a1[0:V_DIM] / a1[V_DIM:V_DIM + 1] - lam * (a2[0:V_DIM] / a2[V_DIM:V_DIM + 1])
        ms = jnp.mean(att * att, axis=0, keepdims=True)
        y = att * lax.rsqrt(ms + EPS) * sg_ref[...]
        o_ref[0, pl.ds(q0, tq), :] = (y.T * ga_ref[0, pl.ds(q0, tq), :]).astype(BF16)

    assert n_k % 2 == 0 and n_k >= 2

    def q_tile_body(qi, cms_diag):
        buf = qi % 2

        def pipe(jj, carry):
            m_run, cms = carry
            i = 2 * jj
            cms_odd = scores(seq(i + 1, qi), qi, buf, 1, False)
            m_run = softmax_pv(seq(i, qi), qi, 0, m_run, cms)
            cms_even = scores(seq(i + 2, qi), qi, buf, 0, False)
            m_run = softmax_pv(seq(i + 1, qi), qi, 1, m_run, cms_odd)
            return m_run, cms_even

        m_run, cms = lax.fori_loop(0, (n_k - 2) // 2, pipe, ((neg_big, neg_big), cms_diag), unroll=True)
        cms_last = scores(seq(n_k - 1, qi), qi, buf, 1, False)
        m_run = softmax_pv(seq(n_k - 2, qi), qi, 0, m_run, cms)
        q_next = jnp.minimum(qi + 1, n_q - 1)
        build_weights(q_next, 1 - buf)
        cms_diag_next = scores(q_next, q_next, 1 - buf, 0, True)
        softmax_pv(seq(n_k - 1, qi), qi, 1, m_run, cms_last)
        finalize(qi)
        return cms_diag_next

    acc1_ref[...] = jnp.zeros_like(acc1_ref)
    acc2_ref[...] = jnp.zeros_like(acc2_ref)
    build_weights(0, 0)
    lax.fori_loop(0, n_q, q_tile_body, scores(0, 0, 0, 0, True), unroll=4)


def _attention(qt, kk, vt, ga, sg, lq1, lk1, lq2, lk2, lambda_init):
    b, n_q, _, tq = qt.shape
    _, n_k, _, tk = vt.shape
    s = kk.shape[1]
    assert tq == Q_TILE and tk == K_TILE and n_q * tq == s and n_k * tk == s
    assert s <= 1 << (POS_SPLIT_BITS + 8)
    full = lambda a: pl.BlockSpec(a.shape, lambda bi, hi: (0,) * a.ndim)
    t_spec = lambda nblk, width: pl.BlockSpec((1, nblk, V_DIM, width), lambda bi, hi: (bi, 0, hi, 0))
    r_spec = pl.BlockSpec((1, s, V_DIM), lambda bi, hi: (bi, 0, hi))
    return pl.pallas_call(
        functools.partial(_attn_kernel, lambda_init=lambda_init),
        grid=(b, N_HEADS),
        in_specs=[t_spec(n_q, tq), r_spec, t_spec(n_k, tk), r_spec,
                  full(sg), full(lq1), full(lk1), full(lq2), full(lk2)],
        out_specs=r_spec,
        out_shape=jax.ShapeDtypeStruct((b, s, ATTN_WIDTH), BF16),
        scratch_shapes=[
            pltpu.VMEM((s, LANES), BF16),
            pltpu.VMEM((2, 3, 2, V_DIM + LANES, Q_TILE), BF16),
            pltpu.VMEM((K_TILE, Q_TILE), F32),
            pltpu.VMEM((K_TILE, Q_TILE), F32),
            pltpu.VMEM((2, 2, K_TILE, Q_TILE), F32),
            pltpu.VMEM((V_DIM + ONES_ROWS, Q_TILE), F32),
            pltpu.VMEM((V_DIM + ONES_ROWS, Q_TILE), F32),
        ],
        compiler_params=pltpu.CompilerParams(
            dimension_semantics=("arbitrary", "arbitrary"), vmem_limit_bytes=VMEM_LIMIT_BYTES),
        name="diff_attn",
    )(qt, kk, vt, ga, sg, lq1, lk1, lq2, lk2)


def _outproj_kernel(x_ref, oa_ref, cg_ref, u_ref, up_ref, un_ref, p_ref, cw_ref, wo32_ref,
                    g2_ref, wg32_ref, wp32_ref, gf_ref, out_ref, wo_ref, wg_ref, wp_ref,
                    *, tiles_per_seq, final_norm):
    i = pl.program_id(0)
    tm = x_ref.shape[0]

    @pl.when(i == 0)
    def _cast_weights_once():
        wo_ref[...] = wo32_ref[...].astype(BF16)
        wg_ref[...] = wg32_ref[...].astype(BF16)
        wp_ref[...] = wp32_ref[...].astype(BF16)

    u = u_ref[...]
    row = lax.broadcasted_iota(jnp.int32, (tm, 1), 0)
    t_in_seq = i % tiles_per_seq
    last = up_ref.shape[0] - 1
    prev_row = jnp.where(t_in_seq == 0, 0.0, up_ref[last:last + 1, :])
    next_row = jnp.where(t_in_seq == tiles_per_seq - 1, 0.0, un_ref[0:1, :])
    u_prev = jnp.where(row == 0, prev_row, pltpu.roll(u, 1, axis=0))
    u_next = jnp.where(row == tm - 1, next_row, pltpu.roll(u, tm - 1, axis=0))
    cw = cw_ref[...]
    conv = cw[0:1] * u_prev + cw[1:2] * u + cw[2:3] * u_next
    o_conv = (cg_ref[...] * conv).astype(BF16)
    aw = oa_ref.shape[1]
    groups = [slice(r0, r0 + OUT_GROUP_ROWS) for r0 in range(0, tm, OUT_GROUP_ROWS)]
    x1 = [x_ref[r, :]
          + jnp.dot(oa_ref[r, :], wo_ref[0:aw, :], preferred_element_type=F32)
          + jnp.dot(o_conv[r], wo_ref[aw:, :], preferred_element_type=F32) for r in groups]
    pp = [jnp.dot(p_ref[r, :].astype(BF16), wp_ref[...], preferred_element_type=F32) for r in groups]
    n1 = [(v * lax.rsqrt(jnp.mean(v * v, axis=-1, keepdims=True) + EPS) * g2_ref[...]).astype(BF16)
          for v in x1]
    gate = [jax.nn.sigmoid(jnp.dot(v, wg_ref[...], preferred_element_type=F32)) for v in n1]
    for r, a, g, q in zip(groups, x1, gate, pp):
        x2 = a + g * q
        if final_norm:
            ms2 = jnp.mean(x2 * x2, axis=-1, keepdims=True)
            x2 = x2 * lax.rsqrt(ms2 + EPS) * gf_ref[...]
        out_ref[r, :] = x2


def _outproj(xf, oa, cg, u, pf, cw, wo, g2, wg, wp, gf, seq_len, final_norm):
    n, d = xf.shape
    tm = OUT_ROW_TILE
    assert seq_len % tm == 0 and tm % OUT_GROUP_ROWS == 0
    nt = n // tm
    sub = 8
    nsub = n // sub
    row_spec = lambda width: pl.BlockSpec((tm, width), lambda i: (i, 0))
    full = lambda a: pl.BlockSpec(a.shape, lambda i: (0,) * a.ndim)
    prev_spec = pl.BlockSpec((sub, CONV_WIDTH), lambda i: (jnp.maximum(i * (tm // sub) - 1, 0), 0))
    next_spec = pl.BlockSpec((sub, CONV_WIDTH), lambda i: (jnp.minimum((i + 1) * (tm // sub), nsub - 1), 0))
    once = lambda a: pl.BlockSpec(a.shape, lambda i: (0,) * a.ndim, pipeline_mode=pl.Buffered(1))
    return pl.pallas_call(
        functools.partial(_outproj_kernel, tiles_per_seq=seq_len // tm, final_norm=final_norm),
        grid=(nt,),
        in_specs=[row_spec(d), row_spec(ATTN_WIDTH), row_spec(CONV_WIDTH), row_spec(CONV_WIDTH),
                  prev_spec, next_spec, row_spec(pf.shape[1]),
                  full(cw), once(wo), full(g2), once(wg), once(wp), full(gf)],
        out_specs=row_spec(d),
        out_shape=jax.ShapeDtypeStruct((n, d), F32),
        scratch_shapes=[pltpu.VMEM(wo.shape, BF16), pltpu.VMEM(wg.shape, BF16),
                        pltpu.VMEM(wp.shape, BF16)],
        compiler_params=pltpu.CompilerParams(
            dimension_semantics=("arbitrary",), vmem_limit_bytes=VMEM_LIMIT_BYTES),
        name="outproj",
    )(xf, oa, cg, u, u, u, pf, cw, wo, g2, wg, wp, gf)


def kernel(x, p, mix_norm_g, w_in, lambda_q1, lambda_k1, lambda_q2, lambda_k2, subln_g, conv_w,
           w_out, ple_norm_g, w_ple_gate, w_ple_proj, final_norm_g):
    b, s, d = x.shape
    depth = w_in.shape[0]
    assert s % ROW_TILE == 0 and ROW_TILE == Q_TILE == K_TILE
    aw = ATTN_WIDTH
    xf = x.reshape(b * s, d)
    gf = final_norm_g.reshape(1, d)
    for i in range(depth):
        lambda_init = 0.8 - 0.6 * math.exp(-0.3 * i)
        kk, ga, cg, u, qt, vt = _inproj(xf, mix_norm_g[i].reshape(1, d), w_in[i],
                                        1.0 - lambda_init, HEAD_DIM ** -0.5 * LOG2E)
        n_t = s // ROW_TILE
        oa = _attention(qt.reshape(b, n_t, aw, ROW_TILE), kk.reshape(b, s, aw),
                        vt.reshape(b, n_t, aw, ROW_TILE), ga.reshape(b, s, aw),
                        subln_g[i].reshape(V_DIM, 1),
                        lambda_q1[i].reshape(1, -1), lambda_k1[i].reshape(1, -1),
                        lambda_q2[i].reshape(1, -1), lambda_k2[i].reshape(1, -1), lambda_init)
        xf = _outproj(xf, oa.reshape(b * s, aw), cg, u, p[i].reshape(b * s, -1), conv_w[i],
                      w_out[i], ple_norm_g[i].reshape(1, d), w_ple_gate[i], w_ple_proj[i], gf, s,
                      final_norm=(i == depth - 1))
    return xf.reshape(b, s, d)
```

```python
import functools
import math

import jax
import jax.numpy as jnp
from jax import lax
from jax.experimental import pallas as pl
from jax.experimental.pallas import tpu as pltpu

F32 = jnp.float32
BF16 = jnp.bfloat16

N_HEADS = 4
HEAD_DIM = 64
V_DIM = 2 * HEAD_DIM
ATTN_WIDTH = N_HEADS * V_DIM
CONV_WIDTH = 512
EPS = 1e-6
LANES = 128

ROW_TILE = 512
IN_GROUP_ROWS = 256
OUT_ROW_TILE = 1024
OUT_GROUP_ROWS = 256
Q_TILE = 512
K_TILE = 512
TILE_GROUP = 4
POS_SPLIT_BITS = 6
SLOPE_PIECES = 3
SLOPE_ROWS = 16
LOG2E = 1.4426950408889634
ONES_ROWS = 16
NEG_BIG = -1e30
VMEM_LIMIT_BYTES = 48 * 1024 * 1024


def _silu(v):
    return v * jax.nn.sigmoid(v)


Q_GROUP, V_GROUP = 0, 2


def _inproj_kernel(x_ref, g_ref, w32_ref, kk_ref, ga_ref, cg_ref, u_ref, qt_ref, vt_ref,
                   wn_ref, wqt_ref, wvt_ref, *, gate_scale, q_scale):
    w = ATTN_WIDTH
    nt = (((1,), (1,)), ((), ()))

    @pl.when(pl.program_id(0) == 0)
    def _relayout_weights_once():
        col = 0
        for grp in range(w32_ref.shape[1] // w):
            blk = w32_ref[:, grp * w:(grp + 1) * w]
            if grp == Q_GROUP:
                wqt_ref[...] = (blk * q_scale).T.astype(BF16)
            elif grp == V_GROUP:
                wvt_ref[...] = blk.T.astype(BF16)
            else:
                wn_ref[:, col:col + w] = blk.astype(BF16)
                col += w

    groups = [slice(r0, r0 + IN_GROUP_ROWS) for r0 in range(0, x_ref.shape[0], IN_GROUP_ROWS)]
    hs = []
    for r in groups:
        x = x_ref[r, :]
        ms = jnp.mean(x * x, axis=-1, keepdims=True)
        hs.append((x * lax.rsqrt(ms + EPS) * g_ref[...]).astype(BF16))
    zns = [jnp.dot(h, wn_ref[...], preferred_element_type=F32) for h in hs]
    qts = [lax.dot_general(wqt_ref[...], h, nt, preferred_element_type=F32) for h in hs]
    vts = [lax.dot_general(wvt_ref[...], h, nt, preferred_element_type=F32) for h in hs]
    for r, zn, qt, vt in zip(groups, zns, qts, vts):
        kk_ref[r, :] = zn[:, 0:w].astype(BF16)
        ga_ref[r, :] = _silu(zn[:, w:2 * w]) * gate_scale
        cb = zn[:, 2 * w:3 * w]
        cc = zn[:, 3 * w:4 * w]
        ch = zn[:, 4 * w:5 * w]
        gc = zn[:, 5 * w:6 * w]
        cg_ref[r, :] = cb * _silu(gc)
        u_ref[r, :] = cc * ch
        qt_ref[0, :, r] = qt.astype(BF16)
        vt_ref[0, :, r] = vt.astype(BF16)


def _inproj(xf, g, wi, gate_scale, q_scale):
    n, d = xf.shape
    tm = ROW_TILE
    nt = n // tm
    w = ATTN_WIDTH
    n_cols = wi.shape[1]
    row_spec = lambda width: pl.BlockSpec((tm, width), lambda i: (i, 0))
    full = lambda a: pl.BlockSpec(a.shape, lambda i: (0,) * a.ndim)
    once = lambda a: pl.BlockSpec(a.shape, lambda i: (0,) * a.ndim, pipeline_mode=pl.Buffered(1))
    t_spec = pl.BlockSpec((1, w, tm), lambda i: (i, 0, 0))
    return pl.pallas_call(
        functools.partial(_inproj_kernel, gate_scale=gate_scale, q_scale=q_scale),
        grid=(nt,),
        in_specs=[row_spec(d), full(g), once(wi)],
        out_specs=[row_spec(w), row_spec(w), row_spec(w), row_spec(w), t_spec, t_spec],
        out_shape=[
            jax.ShapeDtypeStruct((n, w), BF16),
            jax.ShapeDtypeStruct((n, w), F32),
            jax.ShapeDtypeStruct((n, w), F32),
            jax.ShapeDtypeStruct((n, w), F32),
            jax.ShapeDtypeStruct((nt, w, tm), BF16),
            jax.ShapeDtypeStruct((nt, w, tm), BF16),
        ],
        scratch_shapes=[pltpu.VMEM((d, n_cols - 2 * w), BF16),
                        pltpu.VMEM((w, d), BF16),
                        pltpu.VMEM((w, d), BF16)],
        compiler_params=pltpu.CompilerParams(
            dimension_semantics=("arbitrary",), vmem_limit_bytes=VMEM_LIMIT_BYTES),
        name="inproj",
    )(xf, g, wi)


LEFT, DIAG, RIGHT = 0, 1, 2
SIGMA = {LEFT: 1.0, DIAG: 0.0, RIGHT: -1.0}


def _attn_kernel(qt_ref, kk_ref, vt_ref, ga_ref, sg_ref, lq1_ref, lk1_ref, lq2_ref, lk2_ref,
                 o_ref, pos_ref, w_ref, dist_ref, bias_ref, s_ref, acc1_ref, acc2_ref, *, lambda_init):
    n_q = qt_ref.shape[1]
    n_k = vt_ref.shape[1]
    tq, tk = Q_TILE, K_TILE
    head = pl.program_id(1)

    slope_bits = (127 - 2 * (head + 1)) << 23
    slope = lax.bitcast_convert_type(jnp.full((1, 1), slope_bits, jnp.int32), F32)

    lam = (jnp.exp(jnp.sum(lq1_ref[...] * lk1_ref[...], axis=-1, keepdims=True))
           - jnp.exp(jnp.sum(lq2_ref[...] * lk2_ref[...], axis=-1, keepdims=True))
           + lambda_init)

    slope2 = slope * LOG2E

    @pl.when((pl.program_id(0) == 0) & (head == 0))
    def _init_head_independent_scratch():
        s_len = pos_ref.shape[0]
        kpos = lax.broadcasted_iota(jnp.int32, (s_len, LANES), 0)
        lane = lax.broadcasted_iota(jnp.int32, (s_len, LANES), 1)
        hi = ((kpos >> POS_SPLIT_BITS) << POS_SPLIT_BITS).astype(F32)
        lo = (kpos & ((1 << POS_SPLIT_BITS) - 1)).astype(F32)
        pos_ref[...] = jnp.where(lane < SLOPE_PIECES, hi,
                                 jnp.where(lane < 2 * SLOPE_PIECES, lo, 0.0)).astype(BF16)
        di = (lax.broadcasted_iota(jnp.int32, (tk, tq), 0)
              - lax.broadcasted_iota(jnp.int32, (tk, tq), 1))
        dist_ref[...] = jnp.abs(di).astype(F32)
        zeros = jnp.zeros((LANES - SLOPE_ROWS, tq), BF16)
        for buf in range(2):
            for v in (LEFT, DIAG, RIGHT):
                for t in range(2):
                    w_ref[buf, v, t, V_DIM + SLOPE_ROWS:V_DIM + LANES, :] = zeros

    bias_ref[...] = -slope2 * dist_ref[...]

    ones_rows = jnp.where(lax.broadcasted_iota(jnp.int32, (ONES_ROWS, tk), 0) == 0, 1.0, 0.0).astype(BF16)
    q_row = lax.broadcasted_iota(jnp.int32, (V_DIM, tq), 0)
    qlane = lax.broadcasted_iota(jnp.int32, (1, tq), 1)

    neg_big = jnp.full((1, tq), NEG_BIG, F32)
    accs = (acc1_ref, acc2_ref)

    slope_row = lax.broadcasted_iota(jnp.int32, (SLOPE_ROWS, tq), 0)
    rest = jnp.broadcast_to(slope2, (SLOPE_ROWS, tq))
    pieces = []
    for _ in range(SLOPE_PIECES):
        piece = rest.astype(BF16).astype(F32)
        pieces.append(piece)
        rest = rest - piece
    piece_rows = pieces[SLOPE_PIECES - 1]
    for i in range(SLOPE_PIECES - 1):
        piece_rows = jnp.where((slope_row == i) | (slope_row == i + SLOPE_PIECES), pieces[i], piece_rows)
    for v in (LEFT, DIAG, RIGHT):
        rows = jnp.where(slope_row < 2 * SLOPE_PIECES, SIGMA[v] * piece_rows, 0.0).astype(BF16)
        for buf in range(2):
            for t in range(2):
                w_ref[buf, v, t, V_DIM:V_DIM + SLOPE_ROWS, :] = rows

    def build_weights(qi, buf):
        qq = qt_ref[0, qi].astype(F32)
        tops = (jnp.where(q_row < HEAD_DIM, qq, 0.0).astype(BF16),
                jnp.where(q_row >= HEAD_DIM, qq, 0.0).astype(BF16))
        for v in (LEFT, DIAG, RIGHT):
            for t in range(2):
                w_ref[buf, v, t, 0:V_DIM, :] = tops[t]

    def seq(i, qi):
        j = i - 1
        return jnp.where(i == 0, qi, j + jnp.where(j >= qi, 1, 0))

    def shift_of(c, qi):
        side = jnp.where(c < qi, -1.0, jnp.where(c > qi, 1.0, 0.0))
        return side * ((qlane + qi * tq).astype(F32) * slope2)

    def scores(c, qi, buf, slot, diag):
        k0 = c * tk if isinstance(c, int) else pl.multiple_of(c * tk, tk)
        lhs = jnp.concatenate([kk_ref[0, pl.ds(k0, tk), :], pos_ref[pl.ds(k0, tk), :]], axis=1)
        v = DIAG if diag else jnp.where(c < qi, LEFT, RIGHT)
        cms = []
        for t in range(2):
            s = jnp.dot(lhs, w_ref[buf, v, t], preferred_element_type=F32)
            if diag:
                s = s + bias_ref[...]
            s_ref[slot, t] = s
            cm = jnp.max(s, axis=0, keepdims=True)
            cms.append(cm if diag else cm + shift_of(c, qi))
        return tuple(cms)

    def softmax_pv(c, qi, slot, m_run, cms):
        vt = jnp.concatenate([vt_ref[0, c], ones_rows], axis=0)
        shift = shift_of(c, qi)
        new = []
        for t in range(2):
            m_new = jnp.maximum(m_run[t], cms[t])
            alpha = jnp.exp2(m_run[t] - m_new)
            p = jnp.exp2(s_ref[slot, t] - (m_new - shift)).astype(BF16)
            pv = jnp.dot(vt, p, preferred_element_type=F32)
            accs[t][...] = alpha * accs[t][...] + pv
            new.append(m_new)
        return tuple(new)

    def finalize(qi):
        q0 = qi * tq if isinstance(qi, int) else pl.multiple_of(qi * tq, tq)
        a1 = acc1_ref[...]
        a2 = acc2_ref[...]
        att = a1[0:V_DIM] / a1[V_DIM:V_DIM + 1] - lam * (a2[0:V_DIM] / a2[V_DIM:V_DIM + 1])
        ms = jnp.mean(att * att, axis=0, keepdims=True)
        y = att * lax.rsqrt(ms + EPS) * sg_ref[...]
        o_ref[0, pl.ds(q0, tq), :] = (y.T * ga_ref[0, pl.ds(q0, tq), :]).astype(BF16)

    assert n_k % 2 == 0 and n_k >= 2 and n_q % TILE_GROUP == 0

    def q_tile_body(qi, cms_diag, has_next=True):
        buf = qi % 2

        def pipe(jj, carry):
            m_run, cms = carry
            i = 2 * jj
            cms_odd = scores(seq(i + 1, qi), qi, buf, 1, False)
            m_run = softmax_pv(seq(i, qi), qi, 0, m_run, cms)
            cms_even = scores(seq(i + 2, qi), qi, buf, 0, False)
            m_run = softmax_pv(seq(i + 1, qi), qi, 1, m_run, cms_odd)
            return m_run, cms_even

        m_run, cms = lax.fori_loop(0, (n_k - 2) // 2, pipe, ((neg_big, neg_big), cms_diag), unroll=True)
        cms_last = scores(seq(n_k - 1, qi), qi, buf, 1, False)
        m_run = softmax_pv(seq(n_k - 2, qi), qi, 0, m_run, cms)
        cms_diag_next = cms_diag
        if has_next:
            build_weights(qi + 1, 1 - buf)
            cms_diag_next = scores(qi + 1, qi + 1, 1 - buf, 0, True)
        softmax_pv(seq(n_k - 1, qi), qi, 1, m_run, cms_last)
        finalize(qi)
        return cms_diag_next

    acc1_ref[...] = jnp.zeros_like(acc1_ref)
    acc2_ref[...] = jnp.zeros_like(acc2_ref)
    build_weights(0, 0)
    n_loop_groups = (n_q - TILE_GROUP) // TILE_GROUP
    traced_groups = jnp.minimum(pl.program_id(0) + n_loop_groups, n_loop_groups)

    def group_body(g, cms):
        for j in range(TILE_GROUP):
            cms = q_tile_body(g * TILE_GROUP + j, cms)
        return cms

    cms = lax.fori_loop(0, traced_groups, group_body, scores(0, 0, 0, 0, True))
    first_of_last = traced_groups * TILE_GROUP
    for j in range(TILE_GROUP):
        cms = q_tile_body(first_of_last + j, cms, has_next=j < TILE_GROUP - 1)


def _attention(qt, kk, vt, ga, sg, lq1, lk1, lq2, lk2, lambda_init):
    b, n_q, _, tq = qt.shape
    _, n_k, _, tk = vt.shape
    s = kk.shape[1]
    assert tq == Q_TILE and tk == K_TILE and n_q * tq == s and n_k * tk == s
    assert s <= 1 << (POS_SPLIT_BITS + 8)
    full = lambda a: pl.BlockSpec(a.shape, lambda bi, hi: (0,) * a.ndim)
    t_spec = lambda nblk, width: pl.BlockSpec((1, nblk, V_DIM, width), lambda bi, hi: (bi, 0, hi, 0))
    r_spec = pl.BlockSpec((1, s, V_DIM), lambda bi, hi: (bi, 0, hi))
    return pl.pallas_call(
        functools.partial(_attn_kernel, lambda_init=lambda_init),
        grid=(b, N_HEADS),
        in_specs=[t_spec(n_q, tq), r_spec, t_spec(n_k, tk), r_spec,
                  full(sg), full(lq1), full(lk1), full(lq2), full(lk2)],
        out_specs=r_spec,
        out_shape=jax.ShapeDtypeStruct((b, s, ATTN_WIDTH), BF16),
        scratch_shapes=[
            pltpu.VMEM((s, LANES), BF16),
            pltpu.VMEM((2, 3, 2, V_DIM + LANES, Q_TILE), BF16),
            pltpu.VMEM((K_TILE, Q_TILE), F32),
            pltpu.VMEM((K_TILE, Q_TILE), F32),
            pltpu.VMEM((2, 2, K_TILE, Q_TILE), F32),
            pltpu.VMEM((V_DIM + ONES_ROWS, Q_TILE), F32),
            pltpu.VMEM((V_DIM + ONES_ROWS, Q_TILE), F32),
        ],
        compiler_params=pltpu.CompilerParams(
            dimension_semantics=("arbitrary", "arbitrary"), vmem_limit_bytes=VMEM_LIMIT_BYTES),
        name="diff_attn",
    )(qt, kk, vt, ga, sg, lq1, lk1, lq2, lk2)


def _outproj_kernel(x_ref, oa_ref, cg_ref, u_ref, up_ref, un_ref, p_ref, cw_ref, wo32_ref,
                    g2_ref, wg32_ref, wp32_ref, gf_ref, out_ref, wo_ref, wg_ref, wp_ref,
                    *, tiles_per_seq, final_norm):
    i = pl.program_id(0)
    tm = x_ref.shape[0]

    @pl.when(i == 0)
    def _cast_weights_once():
        wo_ref[...] = wo32_ref[...].astype(BF16)
        wg_ref[...] = wg32_ref[...].astype(BF16)
        wp_ref[...] = wp32_ref[...].astype(BF16)

    u = u_ref[...]
    row = lax.broadcasted_iota(jnp.int32, (tm, 1), 0)
    t_in_seq = i % tiles_per_seq
    last = up_ref.shape[0] - 1
    prev_row = jnp.where(t_in_seq == 0, 0.0, up_ref[last:last + 1, :])
    next_row = jnp.where(t_in_seq == tiles_per_seq - 1, 0.0, un_ref[0:1, :])
    u_prev = jnp.where(row == 0, prev_row, pltpu.roll(u, 1, axis=0))
    u_next = jnp.where(row == tm - 1, next_row, pltpu.roll(u, tm - 1, axis=0))
    cw = cw_ref[...]
    conv = cw[0:1] * u_prev + cw[1:2] * u + cw[2:3] * u_next
    o_conv = (cg_ref[...] * conv).astype(BF16)
    aw = oa_ref.shape[1]
    groups = [slice(r0, r0 + OUT_GROUP_ROWS) for r0 in range(0, tm, OUT_GROUP_ROWS)]
    x1 = [x_ref[r, :]
          + jnp.dot(oa_ref[r, :], wo_ref[0:aw, :], preferred_element_type=F32)
          + jnp.dot(o_conv[r], wo_ref[aw:, :], preferred_element_type=F32) for r in groups]
    pp = [jnp.dot(p_ref[r, :].astype(BF16), wp_ref[...], preferred_element_type=F32) for r in groups]
    n1 = [(v * lax.rsqrt(jnp.mean(v * v, axis=-1, keepdims=True) + EPS) * g2_ref[...]).astype(BF16)
          for v in x1]
    gate = [jax.nn.sigmoid(jnp.dot(v, wg_ref[...], preferred_element_type=F32)) for v in n1]
    for r, a, g, q in zip(groups, x1, gate, pp):
        x2 = a + g * q
        if final_norm:
            ms2 = jnp.mean(x2 * x2, axis=-1, keepdims=True)
            x2 = x2 * lax.rsqrt(ms2 + EPS) * gf_ref[...]
        out_ref[r, :] = x2


def _outproj(xf, oa, cg, u, pf, cw, wo, g2, wg, wp, gf, seq_len, final_norm):
    n, d = xf.shape
    tm = OUT_ROW_TILE
    assert seq_len % tm == 0 and tm % OUT_GROUP_ROWS == 0
    nt = n // tm
    sub = 8
    nsub = n // sub
    row_spec = lambda width: pl.BlockSpec((tm, width), lambda i: (i, 0))
    full = lambda a: pl.BlockSpec(a.shape, lambda i: (0,) * a.ndim)
    prev_spec = pl.BlockSpec((sub, CONV_WIDTH), lambda i: (jnp.maximum(i * (tm // sub) - 1, 0), 0))
    next_spec = pl.BlockSpec((sub, CONV_WIDTH), lambda i: (jnp.minimum((i + 1) * (tm // sub), nsub - 1), 0))
    once = lambda a: pl.BlockSpec(a.shape, lambda i: (0,) * a.ndim, pipeline_mode=pl.Buffered(1))
    return pl.pallas_call(
        functools.partial(_outproj_kernel, tiles_per_seq=seq_len // tm, final_norm=final_norm),
        grid=(nt,),
        in_specs=[row_spec(d), row_spec(ATTN_WIDTH), row_spec(CONV_WIDTH), row_spec(CONV_WIDTH),
                  prev_spec, next_spec, row_spec(pf.shape[1]),
                  full(cw), once(wo), full(g2), once(wg), once(wp), full(gf)],
        out_specs=row_spec(d),
        out_shape=jax.ShapeDtypeStruct((n, d), F32),
        scratch_shapes=[pltpu.VMEM(wo.shape, BF16), pltpu.VMEM(wg.shape, BF16),
                        pltpu.VMEM(wp.shape, BF16)],
        compiler_params=pltpu.CompilerParams(
            dimension_semantics=("arbitrary",), vmem_limit_bytes=VMEM_LIMIT_BYTES),
        name="outproj",
    )(xf, oa, cg, u, u, u, pf, cw, wo, g2, wg, wp, gf)


def kernel(x, p, mix_norm_g, w_in, lambda_q1, lambda_k1, lambda_q2, lambda_k2, subln_g, conv_w,
           w_out, ple_norm_g, w_ple_gate, w_ple_proj, final_norm_g):
    b, s, d = x.shape
    depth = w_in.shape[0]
    assert s % ROW_TILE == 0 and ROW_TILE == Q_TILE == K_TILE
    aw = ATTN_WIDTH
    xf = x.reshape(b * s, d)
    gf = final_norm_g.reshape(1, d)
    for i in range(depth):
        lambda_init = 0.8 - 0.6 * math.exp(-0.3 * i)
        kk, ga, cg, u, qt, vt = _inproj(xf, mix_norm_g[i].reshape(1, d), w_in[i],
                                        1.0 - lambda_init, HEAD_DIM ** -0.5 * LOG2E)
        n_t = s // ROW_TILE
        oa = _attention(qt.reshape(b, n_t, aw, ROW_TILE), kk.reshape(b, s, aw),
                        vt.reshape(b, n_t, aw, ROW_TILE), ga.reshape(b, s, aw),
                        subln_g[i].reshape(V_DIM, 1),
                        lambda_q1[i].reshape(1, -1), lambda_k1[i].reshape(1, -1),
                        lambda_q2[i].reshape(1, -1), lambda_k2[i].reshape(1, -1), lambda_init)
        xf = _outproj(xf, oa.reshape(b * s, aw), cg, u, p[i].reshape(b * s, -1), conv_w[i],
                      w_out[i], ple_norm_g[i].reshape(1, d), w_ple_gate[i], w_ple_proj[i], gf, s,
                      final_norm=(i == depth - 1))
    return xf.reshape(b, s, d)
```

```python
import functools
import math

import jax
import jax.numpy as jnp
from jax import lax
from jax.experimental import pallas as pl
from jax.experimental.pallas import tpu as pltpu

F32 = jnp.float32
BF16 = jnp.bfloat16

N_HEADS = 4
HEAD_DIM = 64
V_DIM = 2 * HEAD_DIM
ATTN_WIDTH = N_HEADS * V_DIM
CONV_WIDTH = 512
EPS = 1e-6
LANES = 128

ROW_TILE = 512
IN_GROUP_ROWS = 256
OUT_ROW_TILE = 1024
OUT_GROUP_ROWS = 256
Q_TILE = 512
K_TILE = 512
TILE_GROUP = 4
POS_SPLIT_BITS = 6
SLOPE_PIECES = 3
SLOPE_ROWS = 16
LOG2E = 1.4426950408889634
ONES_ROWS = 16
NEG_BIG = -1e30
VMEM_LIMIT_BYTES = 48 * 1024 * 1024


def _silu(v):
    return v * jax.nn.sigmoid(v)


Q_GROUP, V_GROUP = 0, 2


def _inproj_kernel(x_ref, g_ref, w32_ref, kk_ref, ga_ref, cg_ref, u_ref, qt_ref, vt_ref,
                   wn_ref, wqt_ref, wvt_ref, *, gate_scale, q_scale):
    w = ATTN_WIDTH
    nt = (((1,), (1,)), ((), ()))

    @pl.when(pl.program_id(0) == 0)
    def _relayout_weights_once():
        col = 0
        for grp in range(w32_ref.shape[1] // w):
            blk = w32_ref[:, grp * w:(grp + 1) * w]
            if grp == Q_GROUP:
                wqt_ref[...] = (blk * q_scale).T.astype(BF16)
            elif grp == V_GROUP:
                wvt_ref[...] = blk.T.astype(BF16)
            else:
                wn_ref[:, col:col + w] = blk.astype(BF16)
                col += w

    groups = [slice(r0, r0 + IN_GROUP_ROWS) for r0 in range(0, x_ref.shape[0], IN_GROUP_ROWS)]
    hs = []
    for r in groups:
        x = x_ref[r, :]
        ms = jnp.mean(x * x, axis=-1, keepdims=True)
        hs.append((x * lax.rsqrt(ms + EPS) * g_ref[...]).astype(BF16))
    zns = [jnp.dot(h, wn_ref[...], preferred_element_type=F32) for h in hs]
    qts = [lax.dot_general(wqt_ref[...], h, nt, preferred_element_type=F32) for h in hs]
    vts = [lax.dot_general(wvt_ref[...], h, nt, preferred_element_type=F32) for h in hs]
    for r, zn, qt, vt in zip(groups, zns, qts, vts):
        kk_ref[r, :] = zn[:, 0:w].astype(BF16)
        ga_ref[r, :] = _silu(zn[:, w:2 * w]) * gate_scale
        cb = zn[:, 2 * w:3 * w]
        cc = zn[:, 3 * w:4 * w]
        ch = zn[:, 4 * w:5 * w]
        gc = zn[:, 5 * w:6 * w]
        cg_ref[r, :] = cb * _silu(gc)
        u_ref[r, :] = cc * ch
        qt_ref[0, :, r] = qt.astype(BF16)
        vt_ref[0, :, r] = vt.astype(BF16)


def _inproj(xf, g, wi, gate_scale, q_scale):
    n, d = xf.shape
    tm = ROW_TILE
    nt = n // tm
    w = ATTN_WIDTH
    n_cols = wi.shape[1]
    row_spec = lambda width: pl.BlockSpec((tm, width), lambda i: (i, 0))
    full = lambda a: pl.BlockSpec(a.shape, lambda i: (0,) * a.ndim)
    once = lambda a: pl.BlockSpec(a.shape, lambda i: (0,) * a.ndim, pipeline_mode=pl.Buffered(1))
    t_spec = pl.BlockSpec((1, w, tm), lambda i: (i, 0, 0))
    return pl.pallas_call(
        functools.partial(_inproj_kernel, gate_scale=gate_scale, q_scale=q_scale),
        grid=(nt,),
        in_specs=[row_spec(d), full(g), once(wi)],
        out_specs=[row_spec(w), row_spec(w), row_spec(w), row_spec(w), t_spec, t_spec],
        out_shape=[
            jax.ShapeDtypeStruct((n, w), BF16),
            jax.ShapeDtypeStruct((n, w), F32),
            jax.ShapeDtypeStruct((n, w), F32),
            jax.ShapeDtypeStruct((n, w), F32),
            jax.ShapeDtypeStruct((nt, w, tm), BF16),
            jax.ShapeDtypeStruct((nt, w, tm), BF16),
        ],
        scratch_shapes=[pltpu.VMEM((d, n_cols - 2 * w), BF16),
                        pltpu.VMEM((w, d), BF16),
                        pltpu.VMEM((w, d), BF16)],
        compiler_params=pltpu.CompilerParams(
            dimension_semantics=("arbitrary",), vmem_limit_bytes=VMEM_LIMIT_BYTES),
        name="inproj",
    )(xf, g, wi)


LEFT, DIAG, RIGHT = 0, 1, 2
SIGMA = {LEFT: 1.0, DIAG: 0.0, RIGHT: -1.0}


def _attn_kernel(qt_ref, kk_ref, vt_ref, ga_ref, sg_ref, lq1_ref, lk1_ref, lq2_ref, lk2_ref,
                 o_ref, pos_ref, w_ref, dist_ref, bias_ref, s_ref, acc1_ref, acc2_ref, *, lambda_init):
    n_q = qt_ref.shape[1]
    n_k = vt_ref.shape[1]
    tq, tk = Q_TILE, K_TILE
    head = pl.program_id(1)

    slope_bits = (127 - 2 * (head + 1)) << 23
    slope = lax.bitcast_convert_type(jnp.full((1, 1), slope_bits, jnp.int32), F32)

    lam = (jnp.exp(jnp.sum(lq1_ref[...] * lk1_ref[...], axis=-1, keepdims=True))
           - jnp.exp(jnp.sum(lq2_ref[...] * lk2_ref[...], axis=-1, keepdims=True))
           + lambda_init)

    slope2 = slope * LOG2E

    @pl.when((pl.program_id(0) == 0) & (head == 0))
    def _init_head_independent_scratch():
        s_len = pos_ref.shape[0]
        kpos = lax.broadcasted_iota(jnp.int32, (s_len, LANES), 0)
        lane = lax.broadcasted_iota(jnp.int32, (s_len, LANES), 1)
        hi = ((kpos >> POS_SPLIT_BITS) << POS_SPLIT_BITS).astype(F32)
        lo = (kpos & ((1 << POS_SPLIT_BITS) - 1)).astype(F32)
        pos_ref[...] = jnp.where(lane < SLOPE_PIECES, hi,
                                 jnp.where(lane < 2 * SLOPE_PIECES, lo, 0.0)).astype(BF16)
        di = (lax.broadcasted_iota(jnp.int32, (tk, tq), 0)
              - lax.broadcasted_iota(jnp.int32, (tk, tq), 1))
        dist_ref[...] = jnp.abs(di).astype(F32)
        zeros = jnp.zeros((LANES - SLOPE_ROWS, tq), BF16)
        for buf in range(2):
            for v in (LEFT, DIAG, RIGHT):
                for t in range(2):
                    w_ref[buf, v, t, V_DIM + SLOPE_ROWS:V_DIM + LANES, :] = zeros

    bias_ref[...] = -slope2 * dist_ref[...]

    ones_rows = jnp.where(lax.broadcasted_iota(jnp.int32, (ONES_ROWS, tk), 0) == 0, 1.0, 0.0).astype(BF16)
    q_row = lax.broadcasted_iota(jnp.int32, (V_DIM, tq), 0)
    qlane = lax.broadcasted_iota(jnp.int32, (1, tq), 1)

    neg_big = jnp.full((1, tq), NEG_BIG, F32)
    accs = (acc1_ref, acc2_ref)

    slope_row = lax.broadcasted_iota(jnp.int32, (SLOPE_ROWS, tq), 0)
    rest = jnp.broadcast_to(slope2, (SLOPE_ROWS, tq))
    pieces = []
    for _ in range(SLOPE_PIECES):
        piece = rest.astype(BF16).astype(F32)
        pieces.append(piece)
        rest = rest - piece
    piece_rows = pieces[SLOPE_PIECES - 1]
    for i in range(SLOPE_PIECES - 1):
        piece_rows = jnp.where((slope_row == i) | (slope_row == i + SLOPE_PIECES), pieces[i], piece_rows)
    for v in (LEFT, DIAG, RIGHT):
        rows = jnp.where(slope_row < 2 * SLOPE_PIECES, SIGMA[v] * piece_rows, 0.0).astype(BF16)
        for buf in range(2):
            for t in range(2):
                w_ref[buf, v, t, V_DIM:V_DIM + SLOPE_ROWS, :] = rows

    def build_weights(qi, buf):
        qq = qt_ref[0, qi].astype(F32)
        tops = (jnp.where(q_row < HEAD_DIM, qq, 0.0).astype(BF16),
                jnp.where(q_row >= HEAD_DIM, qq, 0.0).astype(BF16))
        for v in (LEFT, DIAG, RIGHT):
            for t in range(2):
                w_ref[buf, v, t, 0:V_DIM, :] = tops[t]

    def seq(i, qi):
        j = i - 1
        return jnp.where(i == 0, qi, j + jnp.where(j >= qi, 1, 0))

    def shift_of(c, qi):
        side = jnp.where(c < qi, -1.0, jnp.where(c > qi, 1.0, 0.0))
        return side * ((qlane + qi * tq).astype(F32) * slope2)

    def scores(c, qi, buf, slot, diag):
        k0 = c * tk if isinstance(c, int) else pl.multiple_of(c * tk, tk)
        lhs = jnp.concatenate([kk_ref[0, pl.ds(k0, tk), :], pos_ref[pl.ds(k0, tk), :]], axis=1)
        v = DIAG if diag else jnp.where(c < qi, LEFT, RIGHT)
        cms = []
        for t in range(2):
            s = jnp.dot(lhs, w_ref[buf, v, t], preferred_element_type=F32)
            if diag:
                s = s + bias_ref[...]
            s_ref[slot, t] = s
            cm = jnp.max(s, axis=0, keepdims=True)
            cms.append(cm if diag else cm + shift_of(c, qi))
        return tuple(cms)

    def softmax_pv(c, qi, slot, m_run, cms):
        vt = jnp.concatenate([vt_ref[0, c], ones_rows], axis=0)
        shift = shift_of(c, qi)
        new = []
        for t in range(2):
            m_new = jnp.maximum(m_run[t], cms[t])
            alpha = jnp.exp2(m_run[t] - m_new)
            p = jnp.exp2(s_ref[slot, t] - (m_new - shift)).astype(BF16)
            pv = jnp.dot(vt, p, preferred_element_type=F32)
            accs[t][...] = alpha * accs[t][...] + pv
            new.append(m_new)
        return tuple(new)

    def finalize(qi):
        q0 = pl.multiple_of(qi * tq, tq)
        a1 = acc1_ref[...]
        a2 = acc2_ref[...]
        att = a1[0:V_DIM] / a1[V_DIM:V_DIM + 1] - lam * (a2[0:V_DIM] / a2[V_DIM:V_DIM + 1])
        ms = jnp.mean(att * att, axis=0, keepdims=True)
        y = att * lax.rsqrt(ms + EPS) * sg_ref[...]
        o_ref[0, pl.ds(q0, tq), :] = (y.T * ga_ref[0, pl.ds(q0, tq), :]).astype(BF16)

    assert n_k % 2 == 0 and n_k >= 2 and n_q % TILE_GROUP == 0

    def q_tile_body(qi, cms_diag, has_next=True):
        buf = qi % 2

        def pipe(jj, carry):
            m_run, cms = carry
            i = 2 * jj
            cms_odd = scores(seq(i + 1, qi), qi, buf, 1, False)
            m_run = softmax_pv(seq(i, qi), qi, 0, m_run, cms)
            cms_even = scores(seq(i + 2, qi), qi, buf, 0, False)
            m_run = softmax_pv(seq(i + 1, qi), qi, 1, m_run, cms_odd)
            return m_run, cms_even

        m_run, cms = lax.fori_loop(0, (n_k - 2) // 2, pipe, ((neg_big, neg_big), cms_diag), unroll=True)
        cms_last = scores(seq(n_k - 1, qi), qi, buf, 1, False)
        m_run = softmax_pv(seq(n_k - 2, qi), qi, 0, m_run, cms)
        cms_diag_next = cms_diag
        if has_next:
            build_weights(qi + 1, 1 - buf)
            cms_diag_next = scores(qi + 1, qi + 1, 1 - buf, 0, True)
        softmax_pv(seq(n_k - 1, qi), qi, 1, m_run, cms_last)
        finalize(qi)
        return cms_diag_next

    acc1_ref[...] = jnp.zeros_like(acc1_ref)
    acc2_ref[...] = jnp.zeros_like(acc2_ref)
    build_weights(0, 0)

    n_groups = n_q // TILE_GROUP
    traced_zero = jnp.minimum(pl.program_id(0), 0)

    def group_body(last):
        def body(g, cms):
            for j in range(TILE_GROUP):
                cms = q_tile_body(g * TILE_GROUP + j, cms, has_next=not (last and j == TILE_GROUP - 1))
            return cms
        return body

    cms = lax.fori_loop(0, traced_zero + n_groups - 1, group_body(False), scores(0, 0, 0, 0, True))
    lax.fori_loop(traced_zero + n_groups - 1, traced_zero + n_groups, group_body(True), cms)


def _attention(qt, kk, vt, ga, sg, lq1, lk1, lq2, lk2, lambda_init):
    b, n_q, _, tq = qt.shape
    _, n_k, _, tk = vt.shape
    s = kk.shape[1]
    assert tq == Q_TILE and tk == K_TILE and n_q * tq == s and n_k * tk == s
    assert s <= 1 << (POS_SPLIT_BITS + 8)
    full = lambda a: pl.BlockSpec(a.shape, lambda bi, hi: (0,) * a.ndim)
    t_spec = lambda nblk, width: pl.BlockSpec((1, nblk, V_DIM, width), lambda bi, hi: (bi, 0, hi, 0))
    r_spec = pl.BlockSpec((1, s, V_DIM), lambda bi, hi: (bi, 0, hi))
    return pl.pallas_call(
        functools.partial(_attn_kernel, lambda_init=lambda_init),
        grid=(b, N_HEADS),
        in_specs=[t_spec(n_q, tq), r_spec, t_spec(n_k, tk), r_spec,
                  full(sg), full(lq1), full(lk1), full(lq2), full(lk2)],
        out_specs=r_spec,
        out_shape=jax.ShapeDtypeStruct((b, s, ATTN_WIDTH), BF16),
        scratch_shapes=[
            pltpu.VMEM((s, LANES), BF16),
            pltpu.VMEM((2, 3, 2, V_DIM + LANES, Q_TILE), BF16),
            pltpu.VMEM((K_TILE, Q_TILE), F32),
            pltpu.VMEM((K_TILE, Q_TILE), F32),
            pltpu.VMEM((2, 2, K_TILE, Q_TILE), F32),
            pltpu.VMEM((V_DIM + ONES_ROWS, Q_TILE), F32),
            pltpu.VMEM((V_DIM + ONES_ROWS, Q_TILE), F32),
        ],
        compiler_params=pltpu.CompilerParams(
            dimension_semantics=("arbitrary", "arbitrary"), vmem_limit_bytes=VMEM_LIMIT_BYTES),
        name="diff_attn",
    )(qt, kk, vt, ga, sg, lq1, lk1, lq2, lk2)


def _outproj_kernel(x_ref, oa_ref, cg_ref, u_ref, up_ref, un_ref, p_ref, cw_ref, wo32_ref,
                    g2_ref, wg32_ref, wp32_ref, gf_ref, out_ref, wo_ref, wg_ref, wp_ref,
                    *, tiles_per_seq, final_norm):
    i = pl.program_id(0)
    tm = x_ref.shape[0]

    @pl.when(i == 0)
    def _cast_weights_once():
        wo_ref[...] = wo32_ref[...].astype(BF16)
        wg_ref[...] = wg32_ref[...].astype(BF16)
        wp_ref[...] = wp32_ref[...].astype(BF16)

    u = u_ref[...]
    row = lax.broadcasted_iota(jnp.int32, (tm, 1), 0)
    t_in_seq = i % tiles_per_seq
    last = up_ref.shape[0] - 1
    prev_row = jnp.where(t_in_seq == 0, 0.0, up_ref[last:last + 1, :])
    next_row = jnp.where(t_in_seq == tiles_per_seq - 1, 0.0, un_ref[0:1, :])
    u_prev = jnp.where(row == 0, prev_row, pltpu.roll(u, 1, axis=0))
    u_next = jnp.where(row == tm - 1, next_row, pltpu.roll(u, tm - 1, axis=0))
    cw = cw_ref[...]
    conv = cw[0:1] * u_prev + cw[1:2] * u + cw[2:3] * u_next
    o_conv = (cg_ref[...] * conv).astype(BF16)
    aw = oa_ref.shape[1]
    groups = [slice(r0, r0 + OUT_GROUP_ROWS) for r0 in range(0, tm, OUT_GROUP_ROWS)]
    x1 = [x_ref[r, :]
          + jnp.dot(oa_ref[r, :], wo_ref[0:aw, :], preferred_element_type=F32)
          + jnp.dot(o_conv[r], wo_ref[aw:, :], preferred_element_type=F32) for r in groups]
    pp = [jnp.dot(p_ref[r, :].astype(BF16), wp_ref[...], preferred_element_type=F32) for r in groups]
    n1 = [(v * lax.rsqrt(jnp.mean(v * v, axis=-1, keepdims=True) + EPS) * g2_ref[...]).astype(BF16)
          for v in x1]
    gate = [jax.nn.sigmoid(jnp.dot(v, wg_ref[...], preferred_element_type=F32)) for v in n1]
    for r, a, g, q in zip(groups, x1, gate, pp):
        x2 = a + g * q
        if final_norm:
            ms2 = jnp.mean(x2 * x2, axis=-1, keepdims=True)
            x2 = x2 * lax.rsqrt(ms2 + EPS) * gf_ref[...]
        out_ref[r, :] = x2


def _outproj(xf, oa, cg, u, pf, cw, wo, g2, wg, wp, gf, seq_len, final_norm):
    n, d = xf.shape
    tm = OUT_ROW_TILE
    assert seq_len % tm == 0 and tm % OUT_GROUP_ROWS == 0
    nt = n // tm
    sub = 8
    nsub = n // sub
    row_spec = lambda width: pl.BlockSpec((tm, width), lambda i: (i, 0))
    full = lambda a: pl.BlockSpec(a.shape, lambda i: (0,) * a.ndim)
    prev_spec = pl.BlockSpec((sub, CONV_WIDTH), lambda i: (jnp.maximum(i * (tm // sub) - 1, 0), 0))
    next_spec = pl.BlockSpec((sub, CONV_WIDTH), lambda i: (jnp.minimum((i + 1) * (tm // sub), nsub - 1), 0))
    once = lambda a: pl.BlockSpec(a.shape, lambda i: (0,) * a.ndim, pipeline_mode=pl.Buffered(1))
    return pl.pallas_call(
        functools.partial(_outproj_kernel, tiles_per_seq=seq_len // tm, final_norm=final_norm),
        grid=(nt,),
        in_specs=[row_spec(d), row_spec(ATTN_WIDTH), row_spec(CONV_WIDTH), row_spec(CONV_WIDTH),
                  prev_spec, next_spec, row_spec(pf.shape[1]),
                  full(cw), once(wo), full(g2), once(wg), once(wp), full(gf)],
        out_specs=row_spec(d),
        out_shape=jax.ShapeDtypeStruct((n, d), F32),
        scratch_shapes=[pltpu.VMEM(wo.shape, BF16), pltpu.VMEM(wg.shape, BF16),
                        pltpu.VMEM(wp.shape, BF16)],
        compiler_params=pltpu.CompilerParams(
            dimension_semantics=("arbitrary",), vmem_limit_bytes=VMEM_LIMIT_BYTES),
        name="outproj",
    )(xf, oa, cg, u, u, u, pf, cw, wo, g2, wg, wp, gf)


def kernel(x, p, mix_norm_g, w_in, lambda_q1, lambda_k1, lambda_q2, lambda_k2, subln_g, conv_w,
           w_out, ple_norm_g, w_ple_gate, w_ple_proj, final_norm_g):
    b, s, d = x.shape
    depth = w_in.shape[0]
    assert s % ROW_TILE == 0 and ROW_TILE == Q_TILE == K_TILE
    aw = ATTN_WIDTH
    xf = x.reshape(b * s, d)
    gf = final_norm_g.reshape(1, d)
    for i in range(depth):
        lambda_init = 0.8 - 0.6 * math.exp(-0.3 * i)
        kk, ga, cg, u, qt, vt = _inproj(xf, mix_norm_g[i].reshape(1, d), w_in[i],
                                        1.0 - lambda_init, HEAD_DIM ** -0.5 * LOG2E)
        n_t = s // ROW_TILE
        oa = _attention(qt.reshape(b, n_t, aw, ROW_TILE), kk.reshape(b, s, aw),
                        vt.reshape(b, n_t, aw, ROW_TILE), ga.reshape(b, s, aw),
                        subln_g[i].reshape(V_DIM, 1),
                        lambda_q1[i].reshape(1, -1), lambda_k1[i].reshape(1, -1),
                        lambda_q2[i].reshape(1, -1), lambda_k2[i].reshape(1, -1), lambda_init)
        xf = _outproj(xf, oa.reshape(b * s, aw), cg, u, p[i].reshape(b * s, -1), conv_w[i],
                      w_out[i], ple_norm_g[i].reshape(1, d), w_ple_gate[i], w_ple_proj[i], gf, s,
                      final_norm=(i == depth - 1))
    return xf.reshape(b, s, d)
```

```python
import functools
import math

import jax
import jax.numpy as jnp
from jax import lax
from jax.experimental import pallas as pl
from jax.experimental.pallas import tpu as pltpu

F32 = jnp.float32
BF16 = jnp.bfloat16

N_HEADS = 4
HEAD_DIM = 64
V_DIM = 2 * HEAD_DIM
ATTN_WIDTH = N_HEADS * V_DIM
CONV_WIDTH = 512
EPS = 1e-6
LANES = 128

ROW_TILE = 512
IN_GROUP_ROWS = 256
OUT_ROW_TILE = 1024
OUT_GROUP_ROWS = 256
Q_TILE = 512
K_TILE = 512
POS_SPLIT_BITS = 6
SLOPE_PIECES = 3
SLOPE_ROWS = 16
LOG2E = 1.4426950408889634
ONES_ROWS = 16
NEG_BIG = -1e30
VMEM_LIMIT_BYTES = 48 * 1024 * 1024


def _silu(v):
    return v * jax.nn.sigmoid(v)


Q_GROUP, V_GROUP = 0, 2


def _inproj_kernel(x_ref, g_ref, w32_ref, kk_ref, ga_ref, cg_ref, u_ref, qt_ref, vt_ref,
                   wn_ref, wqt_ref, wvt_ref, *, gate_scale, q_scale):
    w = ATTN_WIDTH
    nt = (((1,), (1,)), ((), ()))

    @pl.when(pl.program_id(0) == 0)
    def _relayout_weights_once():
        col = 0
        for grp in range(w32_ref.shape[1] // w):
            blk = w32_ref[:, grp * w:(grp + 1) * w]
            if grp == Q_GROUP:
                wqt_ref[...] = (blk * q_scale).T.astype(BF16)
            elif grp == V_GROUP:
                wvt_ref[...] = blk.T.astype(BF16)
            else:
                wn_ref[:, col:col + w] = blk.astype(BF16)
                col += w

    groups = [slice(r0, r0 + IN_GROUP_ROWS) for r0 in range(0, x_ref.shape[0], IN_GROUP_ROWS)]
    hs = []
    for r in groups:
        x = x_ref[r, :]
        ms = jnp.mean(x * x, axis=-1, keepdims=True)
        hs.append((x * lax.rsqrt(ms + EPS) * g_ref[...]).astype(BF16))
    zns = [jnp.dot(h, wn_ref[...], preferred_element_type=F32) for h in hs]
    qts = [lax.dot_general(wqt_ref[...], h, nt, preferred_element_type=F32) for h in hs]
    vts = [lax.dot_general(wvt_ref[...], h, nt, preferred_element_type=F32) for h in hs]
    for r, zn, qt, vt in zip(groups, zns, qts, vts):
        kk_ref[r, :] = zn[:, 0:w].astype(BF16)
        ga_ref[r, :] = (_silu(zn[:, w:2 * w]) * gate_scale).astype(BF16)
        cb = zn[:, 2 * w:3 * w]
        cc = zn[:, 3 * w:4 * w]
        ch = zn[:, 4 * w:5 * w]
        gc = zn[:, 5 * w:6 * w]
        cg_ref[r, :] = (cb * _silu(gc)).astype(BF16)
        u_ref[r, :] = (cc * ch).astype(BF16)
        qt_ref[0, :, r] = qt.astype(BF16)
        vt_ref[0, :, r] = vt.astype(BF16)


def _inproj(xf, g, wi, gate_scale, q_scale):
    n, d = xf.shape
    tm = ROW_TILE
    nt = n // tm
    w = ATTN_WIDTH
    n_cols = wi.shape[1]
    row_spec = lambda width: pl.BlockSpec((tm, width), lambda i: (i, 0))
    full = lambda a: pl.BlockSpec(a.shape, lambda i: (0,) * a.ndim)
    once = lambda a: pl.BlockSpec(a.shape, lambda i: (0,) * a.ndim, pipeline_mode=pl.Buffered(1))
    t_spec = pl.BlockSpec((1, w, tm), lambda i: (i, 0, 0))
    return pl.pallas_call(
        functools.partial(_inproj_kernel, gate_scale=gate_scale, q_scale=q_scale),
        grid=(nt,),
        in_specs=[row_spec(d), full(g), once(wi)],
        out_specs=[row_spec(w), row_spec(w), row_spec(w), row_spec(w), t_spec, t_spec],
        out_shape=[
            jax.ShapeDtypeStruct((n, w), BF16),
            jax.ShapeDtypeStruct((n, w), BF16),
            jax.ShapeDtypeStruct((n, w), BF16),
            jax.ShapeDtypeStruct((n, w), BF16),
            jax.ShapeDtypeStruct((nt, w, tm), BF16),
            jax.ShapeDtypeStruct((nt, w, tm), BF16),
        ],
        scratch_shapes=[pltpu.VMEM((d, n_cols - 2 * w), BF16),
                        pltpu.VMEM((w, d), BF16),
                        pltpu.VMEM((w, d), BF16)],
        compiler_params=pltpu.CompilerParams(
            dimension_semantics=("arbitrary",), vmem_limit_bytes=VMEM_LIMIT_BYTES),
        name="inproj",
    )(xf, g, wi)


LEFT, DIAG, RIGHT = 0, 1, 2
SIGMA = {LEFT: 1.0, DIAG: 0.0, RIGHT: -1.0}


def _attn_kernel(qt_ref, kk_ref, vt_ref, ga_ref, sg_ref, lq1_ref, lk1_ref, lq2_ref, lk2_ref,
                 o_ref, pos_ref, w_ref, dist_ref, bias_ref, s_ref, acc1_ref, acc2_ref, *, lambda_init):
    n_q = qt_ref.shape[1]
    n_k = vt_ref.shape[1]
    tq, tk = Q_TILE, K_TILE
    head = pl.program_id(1)

    slope_bits = (127 - 2 * (head + 1)) << 23
    slope = lax.bitcast_convert_type(jnp.full((1, 1), slope_bits, jnp.int32), F32)

    lam = (jnp.exp(jnp.sum(lq1_ref[...] * lk1_ref[...], axis=-1, keepdims=True))
           - jnp.exp(jnp.sum(lq2_ref[...] * lk2_ref[...], axis=-1, keepdims=True))
           + lambda_init)

    slope2 = slope * LOG2E

    @pl.when((pl.program_id(0) == 0) & (head == 0))
    def _init_head_independent_scratch():
        s_len = pos_ref.shape[0]
        kpos = lax.broadcasted_iota(jnp.int32, (s_len, LANES), 0)
        lane = lax.broadcasted_iota(jnp.int32, (s_len, LANES), 1)
        hi = ((kpos >> POS_SPLIT_BITS) << POS_SPLIT_BITS).astype(F32)
        lo = (kpos & ((1 << POS_SPLIT_BITS) - 1)).astype(F32)
        pos_ref[...] = jnp.where(lane < SLOPE_PIECES, hi,
                                 jnp.where(lane < 2 * SLOPE_PIECES, lo, 0.0)).astype(BF16)
        di = (lax.broadcasted_iota(jnp.int32, (tk, tq), 0)
              - lax.broadcasted_iota(jnp.int32, (tk, tq), 1))
        dist_ref[...] = jnp.abs(di).astype(F32)
        zeros = jnp.zeros((LANES - SLOPE_ROWS, tq), BF16)
        for buf in range(2):
            for v in (LEFT, DIAG, RIGHT):
                for t in range(2):
                    w_ref[buf, v, t, V_DIM + SLOPE_ROWS:V_DIM + LANES, :] = zeros

    bias_ref[...] = -slope2 * dist_ref[...]

    ones_rows = jnp.where(lax.broadcasted_iota(jnp.int32, (ONES_ROWS, tk), 0) == 0, 1.0, 0.0).astype(BF16)
    q_row = lax.broadcasted_iota(jnp.int32, (V_DIM, tq), 0)
    qlane = lax.broadcasted_iota(jnp.int32, (1, tq), 1)

    neg_big = jnp.full((1, tq), NEG_BIG, F32)
    accs = (acc1_ref, acc2_ref)

    slope_row = lax.broadcasted_iota(jnp.int32, (SLOPE_ROWS, tq), 0)
    rest = jnp.broadcast_to(slope2, (SLOPE_ROWS, tq))
    pieces = []
    for _ in range(SLOPE_PIECES):
        piece = rest.astype(BF16).astype(F32)
        pieces.append(piece)
        rest = rest - piece
    piece_rows = pieces[SLOPE_PIECES - 1]
    for i in range(SLOPE_PIECES - 1):
        piece_rows = jnp.where((slope_row == i) | (slope_row == i + SLOPE_PIECES), pieces[i], piece_rows)
    for v in (LEFT, DIAG, RIGHT):
        rows = jnp.where(slope_row < 2 * SLOPE_PIECES, SIGMA[v] * piece_rows, 0.0).astype(BF16)
        for buf in range(2):
            for t in range(2):
                w_ref[buf, v, t, V_DIM:V_DIM + SLOPE_ROWS, :] = rows

    def build_weights(qi, buf):
        qq = qt_ref[0, qi].astype(F32)
        tops = (jnp.where(q_row < HEAD_DIM, qq, 0.0).astype(BF16),
                jnp.where(q_row >= HEAD_DIM, qq, 0.0).astype(BF16))
        for v in (LEFT, DIAG, RIGHT):
            for t in range(2):
                w_ref[buf, v, t, 0:V_DIM, :] = tops[t]

    def seq(i, qi):
        j = i - 1
        return jnp.where(i == 0, qi, j + jnp.where(j >= qi, 1, 0))

    def shift_of(c, qi):
        side = jnp.where(c < qi, -1.0, jnp.where(c > qi, 1.0, 0.0))
        return side * ((qlane + qi * tq).astype(F32) * slope2)

    def scores(c, qi, buf, slot, diag):
        k0 = c * tk if isinstance(c, int) else pl.multiple_of(c * tk, tk)
        lhs = jnp.concatenate([kk_ref[0, pl.ds(k0, tk), :], pos_ref[pl.ds(k0, tk), :]], axis=1)
        v = DIAG if diag else jnp.where(c < qi, LEFT, RIGHT)
        cms = []
        for t in range(2):
            s = jnp.dot(lhs, w_ref[buf, v, t], preferred_element_type=F32)
            if diag:
                s = s + bias_ref[...]
            s_ref[slot, t] = s
            cm = jnp.max(s, axis=0, keepdims=True)
            cms.append(cm if diag else cm + shift_of(c, qi))
        return tuple(cms)

    def softmax_pv(c, qi, slot, m_run, cms):
        vt = jnp.concatenate([vt_ref[0, c], ones_rows], axis=0)
        shift = shift_of(c, qi)
        new = []
        for t in range(2):
            m_new = jnp.maximum(m_run[t], cms[t])
            alpha = jnp.exp2(m_run[t] - m_new)
            p = jnp.exp2(s_ref[slot, t] - (m_new - shift)).astype(BF16)
            pv = jnp.dot(vt, p, preferred_element_type=F32)
            accs[t][...] = alpha * accs[t][...] + pv
            new.append(m_new)
        return tuple(new)

    def finalize(qi):
        q0 = pl.multiple_of(qi * tq, tq)
        a1 = acc1_ref[...]
        a2 = acc2_ref[...]
        att = a1[0:V_DIM] / a1[V_DIM:V_DIM + 1] - lam * (a2[0:V_DIM] / a2[V_DIM:V_DIM + 1])
        ms = jnp.mean(att * att, axis=0, keepdims=True)
        y = att * lax.rsqrt(ms + EPS) * sg_ref[...]
        o_ref[0, pl.ds(q0, tq), :] = (y.T * ga_ref[0, pl.ds(q0, tq), :].astype(F32)).astype(BF16)

    assert n_k % 2 == 0 and n_k >= 2

    def q_tile_body(qi, cms_diag):
        buf = qi % 2

        def pipe(jj, carry):
            m_run, cms = carry
            i = 2 * jj
            cms_odd = scores(seq(i + 1, qi), qi, buf, 1, False)
            m_run = softmax_pv(seq(i, qi), qi, 0, m_run, cms)
            cms_even = scores(seq(i + 2, qi), qi, buf, 0, False)
            m_run = softmax_pv(seq(i + 1, qi), qi, 1, m_run, cms_odd)
            return m_run, cms_even

        m_run, cms = lax.fori_loop(0, (n_k - 2) // 2, pipe, ((neg_big, neg_big), cms_diag), unroll=True)
        cms_last = scores(seq(n_k - 1, qi), qi, buf, 1, False)
        m_run = softmax_pv(seq(n_k - 2, qi), qi, 0, m_run, cms)
        q_next = jnp.minimum(qi + 1, n_q - 1)
        build_weights(q_next, 1 - buf)
        cms_diag_next = scores(q_next, q_next, 1 - buf, 0, True)
        softmax_pv(seq(n_k - 1, qi), qi, 1, m_run, cms_last)
        finalize(qi)
        return cms_diag_next

    acc1_ref[...] = jnp.zeros_like(acc1_ref)
    acc2_ref[...] = jnp.zeros_like(acc2_ref)
    build_weights(0, 0)
    lax.fori_loop(0, n_q, q_tile_body, scores(0, 0, 0, 0, True), unroll=4)


def _attention(qt, kk, vt, ga, sg, lq1, lk1, lq2, lk2, lambda_init):
    b, n_q, _, tq = qt.shape
    _, n_k, _, tk = vt.shape
    s = kk.shape[1]
    assert tq == Q_TILE and tk == K_TILE and n_q * tq == s and n_k * tk == s
    assert s <= 1 << (POS_SPLIT_BITS + 8)
    full = lambda a: pl.BlockSpec(a.shape, lambda bi, hi: (0,) * a.ndim)
    t_spec = lambda nblk, width: pl.BlockSpec((1, nblk, V_DIM, width), lambda bi, hi: (bi, 0, hi, 0))
    r_spec = pl.BlockSpec((1, s, V_DIM), lambda bi, hi: (bi, 0, hi))
    return pl.pallas_call(
        functools.partial(_attn_kernel, lambda_init=lambda_init),
        grid=(b, N_HEADS),
        in_specs=[t_spec(n_q, tq), r_spec, t_spec(n_k, tk), r_spec,
                  full(sg), full(lq1), full(lk1), full(lq2), full(lk2)],
        out_specs=r_spec,
        out_shape=jax.ShapeDtypeStruct((b, s, ATTN_WIDTH), BF16),
        scratch_shapes=[
            pltpu.VMEM((s, LANES), BF16),
            pltpu.VMEM((2, 3, 2, V_DIM + LANES, Q_TILE), BF16),
            pltpu.VMEM((K_TILE, Q_TILE), F32),
            pltpu.VMEM((K_TILE, Q_TILE), F32),
            pltpu.VMEM((2, 2, K_TILE, Q_TILE), F32),
            pltpu.VMEM((V_DIM + ONES_ROWS, Q_TILE), F32),
            pltpu.VMEM((V_DIM + ONES_ROWS, Q_TILE), F32),
        ],
        compiler_params=pltpu.CompilerParams(
            dimension_semantics=("arbitrary", "arbitrary"), vmem_limit_bytes=VMEM_LIMIT_BYTES),
        name="diff_attn",
    )(qt, kk, vt, ga, sg, lq1, lk1, lq2, lk2)


def _outproj_kernel(x_ref, oa_ref, cg_ref, u_ref, up_ref, un_ref, p_ref, cw_ref, wo32_ref,
                    g2_ref, wg32_ref, wp32_ref, gf_ref, out_ref, wo_ref, wg_ref, wp_ref,
                    *, tiles_per_seq, final_norm):
    i = pl.program_id(0)
    tm = x_ref.shape[0]

    @pl.when(i == 0)
    def _cast_weights_once():
        wo_ref[...] = wo32_ref[...].astype(BF16)
        wg_ref[...] = wg32_ref[...].astype(BF16)
        wp_ref[...] = wp32_ref[...].astype(BF16)

    u = u_ref[...].astype(F32)
    row = lax.broadcasted_iota(jnp.int32, (tm, 1), 0)
    t_in_seq = i % tiles_per_seq
    last = up_ref.shape[0] - 1
    up = up_ref[...].astype(F32)
    un = un_ref[...].astype(F32)
    prev_row = jnp.where(t_in_seq == 0, 0.0, up[last:last + 1, :])
    next_row = jnp.where(t_in_seq == tiles_per_seq - 1, 0.0, un[0:1, :])
    u_prev = jnp.where(row == 0, prev_row, pltpu.roll(u, 1, axis=0))
    u_next = jnp.where(row == tm - 1, next_row, pltpu.roll(u, tm - 1, axis=0))
    cw = cw_ref[...]
    conv = cw[0:1] * u_prev + cw[1:2] * u + cw[2:3] * u_next
    o_conv = (cg_ref[...].astype(F32) * conv).astype(BF16)
    aw = oa_ref.shape[1]
    groups = [slice(r0, r0 + OUT_GROUP_ROWS) for r0 in range(0, tm, OUT_GROUP_ROWS)]
    x1 = [x_ref[r, :]
          + jnp.dot(oa_ref[r, :], wo_ref[0:aw, :], preferred_element_type=F32)
          + jnp.dot(o_conv[r], wo_ref[aw:, :], preferred_element_type=F32) for r in groups]
    pp = [jnp.dot(p_ref[r, :].astype(BF16), wp_ref[...], preferred_element_type=F32) for r in groups]
    n1 = [(v * lax.rsqrt(jnp.mean(v * v, axis=-1, keepdims=True) + EPS) * g2_ref[...]).astype(BF16)
          for v in x1]
    gate = [jax.nn.sigmoid(jnp.dot(v, wg_ref[...], preferred_element_type=F32)) for v in n1]
    for r, a, g, q in zip(groups, x1, gate, pp):
        x2 = a + g * q
        if final_norm:
            ms2 = jnp.mean(x2 * x2, axis=-1, keepdims=True)
            x2 = x2 * lax.rsqrt(ms2 + EPS) * gf_ref[...]
        out_ref[r, :] = x2


def _outproj(xf, oa, cg, u, pf, cw, wo, g2, wg, wp, gf, seq_len, final_norm):
    n, d = xf.shape
    tm = OUT_ROW_TILE
    assert seq_len % tm == 0 and tm % OUT_GROUP_ROWS == 0
    nt = n // tm
    sub = 16
    nsub = n // sub
    row_spec = lambda width: pl.BlockSpec((tm, width), lambda i: (i, 0))
    full = lambda a: pl.BlockSpec(a.shape, lambda i: (0,) * a.ndim)
    prev_spec = pl.BlockSpec((sub, CONV_WIDTH), lambda i: (jnp.maximum(i * (tm // sub) - 1, 0), 0))
    next_spec = pl.BlockSpec((sub, CONV_WIDTH), lambda i: (jnp.minimum((i + 1) * (tm // sub), nsub - 1), 0))
    once = lambda a: pl.BlockSpec(a.shape, lambda i: (0,) * a.ndim, pipeline_mode=pl.Buffered(1))
    return pl.pallas_call(
        functools.partial(_outproj_kernel, tiles_per_seq=seq_len // tm, final_norm=final_norm),
        grid=(nt,),
        in_specs=[row_spec(d), row_spec(ATTN_WIDTH), row_spec(CONV_WIDTH), row_spec(CONV_WIDTH),
                  prev_spec, next_spec, row_spec(pf.shape[1]),
                  full(cw), once(wo), full(g2), once(wg), once(wp), full(gf)],
        out_specs=row_spec(d),
        out_shape=jax.ShapeDtypeStruct((n, d), F32),
        scratch_shapes=[pltpu.VMEM(wo.shape, BF16), pltpu.VMEM(wg.shape, BF16),
                        pltpu.VMEM(wp.shape, BF16)],
        compiler_params=pltpu.CompilerParams(
            dimension_semantics=("arbitrary",), vmem_limit_bytes=VMEM_LIMIT_BYTES),
        name="outproj",
    )(xf, oa, cg, u, u, u, pf, cw, wo, g2, wg, wp, gf)


def kernel(x, p, mix_norm_g, w_in, lambda_q1, lambda_k1, lambda_q2, lambda_k2, subln_g, conv_w,
           w_out, ple_norm_g, w_ple_gate, w_ple_proj, final_norm_g):
    b, s, d = x.shape
    depth = w_in.shape[0]
    assert s % ROW_TILE == 0 and ROW_TILE == Q_TILE == K_TILE
    aw = ATTN_WIDTH
    xf = x.reshape(b * s, d)
    gf = final_norm_g.reshape(1, d)
    for i in range(depth):
        lambda_init = 0.8 - 0.6 * math.exp(-0.3 * i)
        kk, ga, cg, u, qt, vt = _inproj(xf, mix_norm_g[i].reshape(1, d), w_in[i],
                                        1.0 - lambda_init, HEAD_DIM ** -0.5 * LOG2E)
        n_t = s // ROW_TILE
        oa = _attention(qt.reshape(b, n_t, aw, ROW_TILE), kk.reshape(b, s, aw),
                        vt.reshape(b, n_t, aw, ROW_TILE), ga.reshape(b, s, aw),
                        subln_g[i].reshape(V_DIM, 1),
                        lambda_q1[i].reshape(1, -1), lambda_k1[i].reshape(1, -1),
                        lambda_q2[i].reshape(1, -1), lambda_k2[i].reshape(1, -1), lambda_init)
        xf = _outproj(xf, oa.reshape(b * s, aw), cg, u, p[i].reshape(b * s, -1), conv_w[i],
                      w_out[i], ple_norm_g[i].reshape(1, d), w_ple_gate[i], w_ple_proj[i], gf, s,
                      final_norm=(i == depth - 1))
    return xf.reshape(b, s, d)
```

```python
import functools
import math

import jax
import jax.numpy as jnp
from jax import lax
from jax.experimental import pallas as pl
from jax.experimental.pallas import tpu as pltpu

F32 = jnp.float32
BF16 = jnp.bfloat16

N_HEADS = 4
HEAD_DIM = 64
V_DIM = 2 * HEAD_DIM
ATTN_WIDTH = N_HEADS * V_DIM
CONV_WIDTH = 512
EPS = 1e-6
LANES = 128

ROW_TILE = 512
IN_GROUP_ROWS = 256
OUT_ROW_TILE = 1024
OUT_GROUP_ROWS = 256
Q_TILE = 512
K_TILE = 512
POS_SPLIT_BITS = 6
SLOPE_PIECES = 3
SLOPE_ROWS = 16
LOG2E = 1.4426950408889634
ONES_ROWS = 16
NEG_BIG = -1e30
VMEM_LIMIT_BYTES = 48 * 1024 * 1024


def _silu(v):
    return v * jax.nn.sigmoid(v)


Q_GROUP, V_GROUP = 0, 2


def _inproj_kernel(x_ref, g_ref, w32_ref, kk_ref, ga_ref, cg_ref, u_ref, qt_ref, vt_ref,
                   wn_ref, wqt_ref, wvt_ref, *, gate_scale, q_scale):
    w = ATTN_WIDTH
    nt = (((1,), (1,)), ((), ()))

    @pl.when(pl.program_id(0) == 0)
    def _relayout_weights_once():
        col = 0
        for grp in range(w32_ref.shape[1] // w):
            blk = w32_ref[:, grp * w:(grp + 1) * w]
            if grp == Q_GROUP:
                wqt_ref[...] = (blk * q_scale).T.astype(BF16)
            elif grp == V_GROUP:
                wvt_ref[...] = blk.T.astype(BF16)
            else:
                wn_ref[:, col:col + w] = blk.astype(BF16)
                col += w

    groups = [slice(r0, r0 + IN_GROUP_ROWS) for r0 in range(0, x_ref.shape[0], IN_GROUP_ROWS)]
    hs = []
    for r in groups:
        x = x_ref[r, :]
        ms = jnp.mean(x * x, axis=-1, keepdims=True)
        hs.append((x * lax.rsqrt(ms + EPS) * g_ref[...]).astype(BF16))
    zns = [jnp.dot(h, wn_ref[...], preferred_element_type=F32) for h in hs]
    qts = [lax.dot_general(wqt_ref[...], h, nt, preferred_element_type=F32) for h in hs]
    vts = [lax.dot_general(wvt_ref[...], h, nt, preferred_element_type=F32) for h in hs]
    for r, zn, qt, vt in zip(groups, zns, qts, vts):
        kk_ref[r, :] = zn[:, 0:w].astype(BF16)
        ga_ref[r, :] = _silu(zn[:, w:2 * w]) * gate_scale
        cb = zn[:, 2 * w:3 * w]
        cc = zn[:, 3 * w:4 * w]
        ch = zn[:, 4 * w:5 * w]
        gc = zn[:, 5 * w:6 * w]
        cg_ref[r, :] = cb * _silu(gc)
        u_ref[r, :] = cc * ch
        qt_ref[0, :, r] = qt.astype(BF16)
        vt_ref[0, :, r] = vt.astype(BF16)


def _inproj(xf, g, wi, gate_scale, q_scale):
    n, d = xf.shape
    tm = ROW_TILE
    nt = n // tm
    w = ATTN_WIDTH
    n_cols = wi.shape[1]
    row_spec = lambda width: pl.BlockSpec((tm, width), lambda i: (i, 0))
    full = lambda a: pl.BlockSpec(a.shape, lambda i: (0,) * a.ndim)
    once = lambda a: pl.BlockSpec(a.shape, lambda i: (0,) * a.ndim, pipeline_mode=pl.Buffered(1))
    t_spec = pl.BlockSpec((1, w, tm), lambda i: (i, 0, 0))
    return pl.pallas_call(
        functools.partial(_inproj_kernel, gate_scale=gate_scale, q_scale=q_scale),
        grid=(nt,),
        in_specs=[row_spec(d), full(g), once(wi)],
        out_specs=[row_spec(w), row_spec(w), row_spec(w), row_spec(w), t_spec, t_spec],
        out_shape=[
            jax.ShapeDtypeStruct((n, w), BF16),
            jax.ShapeDtypeStruct((n, w), F32),
            jax.ShapeDtypeStruct((n, w), F32),
            jax.ShapeDtypeStruct((n, w), F32),
            jax.ShapeDtypeStruct((nt, w, tm), BF16),
            jax.ShapeDtypeStruct((nt, w, tm), BF16),
        ],
        scratch_shapes=[pltpu.VMEM((d, n_cols - 2 * w), BF16),
                        pltpu.VMEM((w, d), BF16),
                        pltpu.VMEM((w, d), BF16)],
        compiler_params=pltpu.CompilerParams(
            dimension_semantics=("arbitrary",), vmem_limit_bytes=VMEM_LIMIT_BYTES),
        name="inproj",
    )(xf, g, wi)


LEFT, DIAG, RIGHT = 0, 1, 2
SIGMA = {LEFT: 1.0, DIAG: 0.0, RIGHT: -1.0}


def _attn_kernel(qt_ref, kk_ref, vt_ref, ga_ref, sg_ref, lq1_ref, lk1_ref, lq2_ref, lk2_ref,
                 o_ref, pos_ref, w_ref, dist_ref, bias_ref, s_ref, acc1_ref, acc2_ref, *, lambda_init):
    n_q = qt_ref.shape[1]
    n_k = vt_ref.shape[1]
    tq, tk = Q_TILE, K_TILE
    head = pl.program_id(0)
    batch = pl.program_id(1)

    slope_bits = (127 - 2 * (head + 1)) << 23
    slope = lax.bitcast_convert_type(jnp.full((1, 1), slope_bits, jnp.int32), F32)

    lam = (jnp.exp(jnp.sum(lq1_ref[...] * lk1_ref[...], axis=-1, keepdims=True))
           - jnp.exp(jnp.sum(lq2_ref[...] * lk2_ref[...], axis=-1, keepdims=True))
           + lambda_init)

    slope2 = slope * LOG2E

    @pl.when((head == 0) & (batch == 0))
    def _init_head_independent_scratch():
        s_len = pos_ref.shape[0]
        kpos = lax.broadcasted_iota(jnp.int32, (s_len, LANES), 0)
        lane = lax.broadcasted_iota(jnp.int32, (s_len, LANES), 1)
        hi = ((kpos >> POS_SPLIT_BITS) << POS_SPLIT_BITS).astype(F32)
        lo = (kpos & ((1 << POS_SPLIT_BITS) - 1)).astype(F32)
        pos_ref[...] = jnp.where(lane < SLOPE_PIECES, hi,
                                 jnp.where(lane < 2 * SLOPE_PIECES, lo, 0.0)).astype(BF16)
        di = (lax.broadcasted_iota(jnp.int32, (tk, tq), 0)
              - lax.broadcasted_iota(jnp.int32, (tk, tq), 1))
        dist_ref[...] = jnp.abs(di).astype(F32)
        zeros = jnp.zeros((LANES - SLOPE_ROWS, tq), BF16)
        for buf in range(2):
            for v in (LEFT, DIAG, RIGHT):
                for t in range(2):
                    w_ref[buf, v, t, V_DIM + SLOPE_ROWS:V_DIM + LANES, :] = zeros

    ones_rows = jnp.where(lax.broadcasted_iota(jnp.int32, (ONES_ROWS, tk), 0) == 0, 1.0, 0.0).astype(BF16)
    q_row = lax.broadcasted_iota(jnp.int32, (V_DIM, tq), 0)
    qlane = lax.broadcasted_iota(jnp.int32, (1, tq), 1)

    neg_big = jnp.full((1, tq), NEG_BIG, F32)
    accs = (acc1_ref, acc2_ref)

    @pl.when(batch == 0)
    def _per_head_tables():
        bias_ref[...] = -slope2 * dist_ref[...]
        slope_row = lax.broadcasted_iota(jnp.int32, (SLOPE_ROWS, tq), 0)
        rest = jnp.broadcast_to(slope2, (SLOPE_ROWS, tq))
        pieces = []
        for _ in range(SLOPE_PIECES):
            piece = rest.astype(BF16).astype(F32)
            pieces.append(piece)
            rest = rest - piece
        piece_rows = pieces[SLOPE_PIECES - 1]
        for i in range(SLOPE_PIECES - 1):
            piece_rows = jnp.where((slope_row == i) | (slope_row == i + SLOPE_PIECES), pieces[i], piece_rows)
        for v in (LEFT, DIAG, RIGHT):
            rows = jnp.where(slope_row < 2 * SLOPE_PIECES, SIGMA[v] * piece_rows, 0.0).astype(BF16)
            for buf in range(2):
                for t in range(2):
                    w_ref[buf, v, t, V_DIM:V_DIM + SLOPE_ROWS, :] = rows

    def build_weights(qi, buf):
        qq = qt_ref[0, qi].astype(F32)
        tops = (jnp.where(q_row < HEAD_DIM, qq, 0.0).astype(BF16),
                jnp.where(q_row >= HEAD_DIM, qq, 0.0).astype(BF16))
        for v in (LEFT, DIAG, RIGHT):
            for t in range(2):
                w_ref[buf, v, t, 0:V_DIM, :] = tops[t]

    def seq(i, qi):
        j = i - 1
        return jnp.where(i == 0, qi, j + jnp.where(j >= qi, 1, 0))

    def shift_of(c, qi):
        side = jnp.where(c < qi, -1.0, jnp.where(c > qi, 1.0, 0.0))
        return side * ((qlane + qi * tq).astype(F32) * slope2)

    def scores(c, qi, buf, slot, diag):
        k0 = c * tk if isinstance(c, int) else pl.multiple_of(c * tk, tk)
        lhs = jnp.concatenate([kk_ref[0, pl.ds(k0, tk), :], pos_ref[pl.ds(k0, tk), :]], axis=1)
        v = DIAG if diag else jnp.where(c < qi, LEFT, RIGHT)
        cms = []
        for t in range(2):
            s = jnp.dot(lhs, w_ref[buf, v, t], preferred_element_type=F32)
            if diag:
                s = s + bias_ref[...]
            s_ref[slot, t] = s
            cm = jnp.max(s, axis=0, keepdims=True)
            cms.append(cm if diag else cm + shift_of(c, qi))
        return tuple(cms)

    def softmax_pv(c, qi, slot, m_run, cms):
        vt = jnp.concatenate([vt_ref[0, c], ones_rows], axis=0)
        shift = shift_of(c, qi)
        new = []
        for t in range(2):
            m_new = jnp.maximum(m_run[t], cms[t])
            alpha = jnp.exp2(m_run[t] - m_new)
            p = jnp.exp2(s_ref[slot, t] - (m_new - shift)).astype(BF16)
            pv = jnp.dot(vt, p, preferred_element_type=F32)
            accs[t][...] = alpha * accs[t][...] + pv
            new.append(m_new)
        return tuple(new)

    def finalize(qi):
        q0 = pl.multiple_of(qi * tq, tq)
        a1 = acc1_ref[...]
        a2 = acc2_ref[...]
        att = a1[0:V_DIM] / a1[V_DIM:V_DIM + 1] - lam * (a2[0:V_DIM] / a2[V_DIM:V_DIM + 1])
        ms = jnp.mean(att * att, axis=0, keepdims=True)
        y = att * lax.rsqrt(ms + EPS) * sg_ref[...]
        o_ref[0, pl.ds(q0, tq), :] = (y.T * ga_ref[0, pl.ds(q0, tq), :]).astype(BF16)

    assert n_k % 2 == 0 and n_k >= 2

    def q_tile_body(qi, cms_diag):
        buf = qi % 2

        def pipe(jj, carry):
            m_run, cms = carry
            i = 2 * jj
            cms_odd = scores(seq(i + 1, qi), qi, buf, 1, False)
            m_run = softmax_pv(seq(i, qi), qi, 0, m_run, cms)
            cms_even = scores(seq(i + 2, qi), qi, buf, 0, False)
            m_run = softmax_pv(seq(i + 1, qi), qi, 1, m_run, cms_odd)
            return m_run, cms_even

        m_run, cms = lax.fori_loop(0, (n_k - 2) // 2, pipe, ((neg_big, neg_big), cms_diag), unroll=True)
        cms_last = scores(seq(n_k - 1, qi), qi, buf, 1, False)
        m_run = softmax_pv(seq(n_k - 2, qi), qi, 0, m_run, cms)
        q_next = jnp.minimum(qi + 1, n_q - 1)
        build_weights(q_next, 1 - buf)
        cms_diag_next = scores(q_next, q_next, 1 - buf, 0, True)
        softmax_pv(seq(n_k - 1, qi), qi, 1, m_run, cms_last)
        finalize(qi)
        return cms_diag_next

    acc1_ref[...] = jnp.zeros_like(acc1_ref)
    acc2_ref[...] = jnp.zeros_like(acc2_ref)
    build_weights(0, 0)
    lax.fori_loop(0, n_q, q_tile_body, scores(0, 0, 0, 0, True), unroll=4)


def _attention(qt, kk, vt, ga, sg, lq1, lk1, lq2, lk2, lambda_init):
    b, n_q, _, tq = qt.shape
    _, n_k, _, tk = vt.shape
    s = kk.shape[1]
    assert tq == Q_TILE and tk == K_TILE and n_q * tq == s and n_k * tk == s
    assert s <= 1 << (POS_SPLIT_BITS + 8)
    full = lambda a: pl.BlockSpec(a.shape, lambda hi, bi: (0,) * a.ndim)
    t_spec = lambda nblk, width: pl.BlockSpec((1, nblk, V_DIM, width), lambda hi, bi: (bi, 0, hi, 0))
    r_spec = pl.BlockSpec((1, s, V_DIM), lambda hi, bi: (bi, 0, hi))
    return pl.pallas_call(
        functools.partial(_attn_kernel, lambda_init=lambda_init),
        grid=(N_HEADS, b),
        in_specs=[t_spec(n_q, tq), r_spec, t_spec(n_k, tk), r_spec,
                  full(sg), full(lq1), full(lk1), full(lq2), full(lk2)],
        out_specs=r_spec,
        out_shape=jax.ShapeDtypeStruct((b, s, ATTN_WIDTH), BF16),
        scratch_shapes=[
            pltpu.VMEM((s, LANES), BF16),
            pltpu.VMEM((2, 3, 2, V_DIM + LANES, Q_TILE), BF16),
            pltpu.VMEM((K_TILE, Q_TILE), F32),
            pltpu.VMEM((K_TILE, Q_TILE), F32),
            pltpu.VMEM((2, 2, K_TILE, Q_TILE), F32),
            pltpu.VMEM((V_DIM + ONES_ROWS, Q_TILE), F32),
            pltpu.VMEM((V_DIM + ONES_ROWS, Q_TILE), F32),
        ],
        compiler_params=pltpu.CompilerParams(
            dimension_semantics=("arbitrary", "arbitrary"), vmem_limit_bytes=VMEM_LIMIT_BYTES),
        name="diff_attn",
    )(qt, kk, vt, ga, sg, lq1, lk1, lq2, lk2)


def _outproj_kernel(x_ref, oa_ref, cg_ref, u_ref, up_ref, un_ref, p_ref, cw_ref, wo32_ref,
                    g2_ref, wg32_ref, wp32_ref, gf_ref, out_ref, wo_ref, wg_ref, wp_ref,
                    *, tiles_per_seq, final_norm):
    i = pl.program_id(0)
    tm = x_ref.shape[0]

    @pl.when(i == 0)
    def _cast_weights_once():
        wo_ref[...] = wo32_ref[...].astype(BF16)
        wg_ref[...] = wg32_ref[...].astype(BF16)
        wp_ref[...] = wp32_ref[...].astype(BF16)

    u = u_ref[...]
    row = lax.broadcasted_iota(jnp.int32, (tm, 1), 0)
    t_in_seq = i % tiles_per_seq
    last = up_ref.shape[0] - 1
    prev_row = jnp.where(t_in_seq == 0, 0.0, up_ref[last:last + 1, :])
    next_row = jnp.where(t_in_seq == tiles_per_seq - 1, 0.0, un_ref[0:1, :])
    u_prev = jnp.where(row == 0, prev_row, pltpu.roll(u, 1, axis=0))
    u_next = jnp.where(row == tm - 1, next_row, pltpu.roll(u, tm - 1, axis=0))
    cw = cw_ref[...]
    conv = cw[0:1] * u_prev + cw[1:2] * u + cw[2:3] * u_next
    o_conv = (cg_ref[...] * conv).astype(BF16)
    aw = oa_ref.shape[1]
    groups = [slice(r0, r0 + OUT_GROUP_ROWS) for r0 in range(0, tm, OUT_GROUP_ROWS)]
    x1 = [x_ref[r, :]
          + jnp.dot(oa_ref[r, :], wo_ref[0:aw, :], preferred_element_type=F32)
          + jnp.dot(o_conv[r], wo_ref[aw:, :], preferred_element_type=F32) for r in groups]
    pp = [jnp.dot(p_ref[r, :].astype(BF16), wp_ref[...], preferred_element_type=F32) for r in groups]
    n1 = [(v * lax.rsqrt(jnp.mean(v * v, axis=-1, keepdims=True) + EPS) * g2_ref[...]).astype(BF16)
          for v in x1]
    gate = [jax.nn.sigmoid(jnp.dot(v, wg_ref[...], preferred_element_type=F32)) for v in n1]
    for r, a, g, q in zip(groups, x1, gate, pp):
        x2 = a + g * q
        if final_norm:
            ms2 = jnp.mean(x2 * x2, axis=-1, keepdims=True)
            x2 = x2 * lax.rsqrt(ms2 + EPS) * gf_ref[...]
        out_ref[r, :] = x2


def _outproj(xf, oa, cg, u, pf, cw, wo, g2, wg, wp, gf, seq_len, final_norm):
    n, d = xf.shape
    tm = OUT_ROW_TILE
    assert seq_len % tm == 0 and tm % OUT_GROUP_ROWS == 0
    nt = n // tm
    sub = 8
    nsub = n // sub
    row_spec = lambda width: pl.BlockSpec((tm, width), lambda i: (i, 0))
    full = lambda a: pl.BlockSpec(a.shape, lambda i: (0,) * a.ndim)
    prev_spec = pl.BlockSpec((sub, CONV_WIDTH), lambda i: (jnp.maximum(i * (tm // sub) - 1, 0), 0))
    next_spec = pl.BlockSpec((sub, CONV_WIDTH), lambda i: (jnp.minimum((i + 1) * (tm // sub), nsub - 1), 0))
    once = lambda a: pl.BlockSpec(a.shape, lambda i: (0,) * a.ndim, pipeline_mode=pl.Buffered(1))
    return pl.pallas_call(
        functools.partial(_outproj_kernel, tiles_per_seq=seq_len // tm, final_norm=final_norm),
        grid=(nt,),
        in_specs=[row_spec(d), row_spec(ATTN_WIDTH), row_spec(CONV_WIDTH), row_spec(CONV_WIDTH),
                  prev_spec, next_spec, row_spec(pf.shape[1]),
                  full(cw), once(wo), full(g2), once(wg), once(wp), full(gf)],
        out_specs=row_spec(d),
        out_shape=jax.ShapeDtypeStruct((n, d), F32),
        scratch_shapes=[pltpu.VMEM(wo.shape, BF16), pltpu.VMEM(wg.shape, BF16),
                        pltpu.VMEM(wp.shape, BF16)],
        compiler_params=pltpu.CompilerParams(
            dimension_semantics=("arbitrary",), vmem_limit_bytes=VMEM_LIMIT_BYTES),
        name="outproj",
    )(xf, oa, cg, u, u, u, pf, cw, wo, g2, wg, wp, gf)


def kernel(x, p, mix_norm_g, w_in, lambda_q1, lambda_k1, lambda_q2, lambda_k2, subln_g, conv_w,
           w_out, ple_norm_g, w_ple_gate, w_ple_proj, final_norm_g):
    b, s, d = x.shape
    depth = w_in.shape[0]
    assert s % ROW_TILE == 0 and ROW_TILE == Q_TILE == K_TILE
    aw = ATTN_WIDTH
    xf = x.reshape(b * s, d)
    gf = final_norm_g.reshape(1, d)
    for i in range(depth):
        lambda_init = 0.8 - 0.6 * math.exp(-0.3 * i)
        kk, ga, cg, u, qt, vt = _inproj(xf, mix_norm_g[i].reshape(1, d), w_in[i],
                                        1.0 - lambda_init, HEAD_DIM ** -0.5 * LOG2E)
        n_t = s // ROW_TILE
        oa = _attention(qt.reshape(b, n_t, aw, ROW_TILE), kk.reshape(b, s, aw),
                        vt.reshape(b, n_t, aw, ROW_TILE), ga.reshape(b, s, aw),
                        subln_g[i].reshape(V_DIM, 1),
                        lambda_q1[i].reshape(1, -1), lambda_k1[i].reshape(1, -1),
                        lambda_q2[i].reshape(1, -1), lambda_k2[i].reshape(1, -1), lambda_init)
        xf = _outproj(xf, oa.reshape(b * s, aw), cg, u, p[i].reshape(b * s, -1), conv_w[i],
                      w_out[i], ple_norm_g[i].reshape(1, d), w_ple_gate[i], w_ple_proj[i], gf, s,
                      final_norm=(i == depth - 1))
    return xf.reshape(b, s, d)
```
